```python
import jax, jax.numpy as jnp
from jax import lax
import numpy as np

D_MODEL = 1024
BATCH = 2
SEQ = 8192
DEPTH = 4
DEC_BATCH = 32
DEC_SEQ = 4
PAST_LEN = 8192
PAGE_SIZE = 128

N_HEADS_A = 8
HEAD_DIM = 64
D_A = N_HEADS_A * HEAD_DIM
N_GROUPS_B = 4
D_B = D_MODEL // 4
N_GROUPS_C = 4
D_C = D_MODEL // 4
D_MIX = D_A + D_B + D_C
D_IN = 3 * D_A + 3 * D_B + 2 * D_C
B_CONV = 3
C_CONV = 31
D_FF = 4 * D_MODEL
FF_CONV = 3
D_PLE = 256
Q_BLOCK = 128
SB_BIAS_INIT = -7.0
EPS = 1e-6

kernel_name = "hybrid_stickbreak_conv_decoder_step"


def rmsnorm(x, g):
    xf = x.astype(jnp.float32)
    y = xf * lax.rsqrt(jnp.mean(xf * xf, axis=-1, keepdims=True) + EPS) * g.astype(jnp.float32)
    return y.astype(x.dtype)


def layernorm(x, g, b):
    xf = x.astype(jnp.float32)
    mu = jnp.mean(xf, axis=-1, keepdims=True)
    var = jnp.mean(jnp.square(xf - mu), axis=-1, keepdims=True)
    y = (xf - mu) * lax.rsqrt(var + EPS) * g.astype(jnp.float32) + b.astype(jnp.float32)
    return y.astype(x.dtype)


def causal_dwconv(u, w, prev):
    width, ch = w.shape
    xp = jnp.concatenate([prev.astype(u.dtype), u], axis=1)
    y = lax.conv_general_dilated(xp, w[:, None, :].astype(u.dtype), window_strides=(1,),
                                 padding='VALID', dimension_numbers=('NWC', 'WIO', 'NWC'),
                                 feature_group_count=ch)
    return y, xp[:, xp.shape[1] - (width - 1):]


def sb_attend(q, k, v, q_pos, k_pos, bias):
    scale = HEAD_DIM ** -0.5
    z = (jnp.einsum('bqhd,bkhd->bhqk', q.astype(jnp.float32), k.astype(jnp.float32)) * scale
         + bias.astype(jnp.float32)[None, :, None, None])
    mask = k_pos[None, :] < q_pos[:, None]
    log_1mb = jnp.where(mask, jax.nn.log_sigmoid(-z), 0.0)
    after = lax.cumsum(log_1mb, axis=3, reverse=True) - log_1mb
    w = jnp.where(mask, jnp.exp(jax.nn.log_sigmoid(z) + after), 0.0)
    o = jnp.einsum('bhqk,bkhd->bqhd', w, v.astype(jnp.float32))
    return o.astype(v.dtype)


def mixer(h, w_in, w_o, sb_bias, b_conv_w, c_conv_w, c_conv_b, c_ln_g, c_ln_b,
          k_past, v_past, b_prev, c_prev, blocked):
    bsz, t, _ = h.shape
    zz = h @ w_in
    o1 = D_A; o2 = 2 * D_A; o3 = 3 * D_A
    o4 = o3 + D_B; o5 = o4 + D_B; o6 = o5 + D_B; o7 = o6 + D_C
    q, k, v, bg, cg, bx, ca, cgt = jnp.split(zz, [o1, o2, o3, o4, o5, o6, o7], axis=-1)
    q = q.reshape(bsz, t, N_HEADS_A, HEAD_DIM)
    k = k.reshape(bsz, t, N_HEADS_A, HEAD_DIM)
    v = v.reshape(bsz, t, N_HEADS_A, HEAD_DIM)

    if k_past is None:
        k_all, v_all = k, v
        past = 0
    else:
        k_all = jnp.concatenate([k_past.astype(k.dtype), k], axis=1)
        v_all = jnp.concatenate([v_past.astype(v.dtype), v], axis=1)
        past = k_past.shape[1]
    q_pos = past + jnp.arange(t)
    k_pos = jnp.arange(past + t)
    if blocked:
        nb = t // Q_BLOCK
        qb = q.reshape(bsz, nb, Q_BLOCK, N_HEADS_A, HEAD_DIM).transpose(1, 0, 2, 3, 4)
        qp = q_pos.reshape(nb, Q_BLOCK)
        ob = lax.map(lambda a: sb_attend(a[0], k_all, v_all, a[1], k_pos, sb_bias), (qb, qp))
        o_a = ob.transpose(1, 0, 2, 3, 4).reshape(bsz, t, D_A)
    else:
        o_a = sb_attend(q, k_all, v_all, q_pos, k_pos, sb_bias).reshape(bsz, t, D_A)

    u = cg * bx
    yb, b_state = causal_dwconv(u, b_conv_w, b_prev)
    o_b = bg * yb

    g = ca * jax.nn.sigmoid(cgt)
    yc, c_state = causal_dwconv(g, c_conv_w, c_prev)
    yc = yc + c_conv_b.astype(yc.dtype)
    o_c = jax.nn.silu(layernorm(yc, c_ln_g, c_ln_b))

    out = jnp.concatenate([o_a, o_b, o_c], axis=-1) @ w_o
    return out, k, v, b_state, c_state


def conv_ffn(h, w_gate, w_up, w_down, ff_conv_w, f_prev):
    gt, f_state = causal_dwconv(h @ w_gate, ff_conv_w, f_prev)
    y = (jax.nn.gelu(gt, approximate=True) * (h @ w_up)) @ w_down
    return y, f_state


def trunk(x, pe, kv_cache, b_prevs, c_prevs, f_prevs, blocked,
          w_in, w_o, sb_bias, b_conv_w, c_conv_w, c_conv_b, c_ln_g, c_ln_b,
          w_gate, w_up, w_down, ff_conv_w, w_ple, w_ple_gate, g_norm):
    ks, vs, bs, cs, fs = [], [], [], [], []
    for i in range(DEPTH):
        if kv_cache is None:
            k_past = v_past = None
        else:
            cache_k, cache_v, page_table = kv_cache
            nseq, npg = page_table.shape
            k_past = cache_k[i][page_table].reshape(nseq, npg * PAGE_SIZE, N_HEADS_A, HEAD_DIM)
            v_past = cache_v[i][page_table].reshape(nseq, npg * PAGE_SIZE, N_HEADS_A, HEAD_DIM)
        h = rmsnorm(x, g_norm[i, 0])
        m, k, v, b_st, c_st = mixer(h, w_in[i], w_o[i], sb_bias[i], b_conv_w[i], c_conv_w[i],
                                    c_conv_b[i], c_ln_g[i], c_ln_b[i], k_past, v_past,
                                    b_prevs[i], c_prevs[i], blocked)
        x = x + rmsnorm(m, g_norm[i, 1])
        f, f_st = conv_ffn(rmsnorm(x, g_norm[i, 2]), w_gate[i], w_up[i], w_down[i],
                           ff_conv_w[i], f_prevs[i])
        x = x + rmsnorm(f, g_norm[i, 3])
        e = rmsnorm(pe[i].astype(x.dtype) @ w_ple[i], g_norm[i, 4])
        x = x + e * jax.nn.sigmoid(x @ w_ple_gate[i])
        ks.append(k); vs.append(v); bs.append(b_st); cs.append(c_st); fs.append(f_st)
    return (x, jnp.stack(ks), jnp.stack(vs), jnp.stack(bs), jnp.stack(cs), jnp.stack(fs))


def setup_inputs(seed: int = 0) -> dict:
    key = jax.random.key(seed)
    ks = jax.random.split(key, 32)
    n_pages = PAST_LEN // PAGE_SIZE
    n_used = DEC_BATCH * n_pages
    n_pool = n_used + max(1, n_used // 4)
    nrm = lambda k, s, sc=1.0: jax.random.normal(k, s, jnp.float32) * sc
    page_table = jax.random.permutation(ks[0], n_pool)[:n_used].reshape(DEC_BATCH, n_pages).astype(jnp.int32)
    return {
        "x_prompt": nrm(ks[1], (BATCH, SEQ, D_MODEL)),
        "x_sample": nrm(ks[2], (DEC_BATCH, DEC_SEQ, D_MODEL)),
        "p_prompt": nrm(ks[3], (DEPTH, BATCH, SEQ, D_PLE)),
        "p_sample": nrm(ks[4], (DEPTH, DEC_BATCH, DEC_SEQ, D_PLE)),
        "cache_k": nrm(ks[5], (DEPTH, n_pool, PAGE_SIZE, N_HEADS_A, HEAD_DIM)),
        "cache_v": nrm(ks[6], (DEPTH, n_pool, PAGE_SIZE, N_HEADS_A, HEAD_DIM)),
        "page_table": page_table,
        "state_bconv": nrm(ks[7], (DEPTH, DEC_BATCH, B_CONV - 1, D_B)),
        "state_cconv": nrm(ks[8], (DEPTH, DEC_BATCH, C_CONV - 1, D_C)),
        "state_ffconv": nrm(ks[9], (DEPTH, DEC_BATCH, FF_CONV - 1, D_FF)),
        "w_in": nrm(ks[10], (DEPTH, D_MODEL, D_IN), D_MODEL ** -0.5),
        "w_o": nrm(ks[11], (DEPTH, D_MIX, D_MODEL), D_MIX ** -0.5),
        "sb_bias": SB_BIAS_INIT + nrm(ks[24], (DEPTH, N_HEADS_A), 0.5),
        "b_conv_w": nrm(ks[12], (DEPTH, B_CONV, D_B), B_CONV ** -0.5),
        "c_conv_w": nrm(ks[13], (DEPTH, C_CONV, D_C), C_CONV ** -0.5),
        "c_conv_b": nrm(ks[14], (DEPTH, D_C), 0.01),
        "c_ln_g": 1.0 + nrm(ks[15], (DEPTH, D_C), 0.05),
        "c_ln_b": nrm(ks[16], (DEPTH, D_C), 0.01),
        "w_gate": nrm(ks[17], (DEPTH, D_MODEL, D_FF), D_MODEL ** -0.5),
        "w_up": nrm(ks[18], (DEPTH, D_MODEL, D_FF), D_MODEL ** -0.5),
        "w_down": nrm(ks[19], (DEPTH, D_FF, D_MODEL), D_FF ** -0.5),
        "ff_conv_w": nrm(ks[20], (DEPTH, FF_CONV, D_FF), FF_CONV ** -0.5),
        "w_ple": nrm(ks[21], (DEPTH, D_PLE, D_MODEL), D_PLE ** -0.5),
        "w_ple_gate": nrm(ks[22], (DEPTH, D_MODEL, D_MODEL), D_MODEL ** -0.5),
        "g_norm": 1.0 + nrm(ks[23], (DEPTH, 5, D_MODEL), 0.05),
    }


def reference(x_prompt, x_sample, p_prompt, p_sample, cache_k, cache_v, page_table,
              state_bconv, state_cconv, state_ffconv,
              w_in, w_o, sb_bias, b_conv_w, c_conv_w, c_conv_b, c_ln_g, c_ln_b,
              w_gate, w_up, w_down, ff_conv_w, w_ple, w_ple_gate, g_norm):
    dt = x_prompt.dtype
    bp = x_prompt.shape[0]
    z_b = jnp.zeros((DEPTH, bp, B_CONV - 1, D_B), dt)
    z_c = jnp.zeros((DEPTH, bp, C_CONV - 1, D_C), dt)
    z_f = jnp.zeros((DEPTH, bp, FF_CONV - 1, D_FF), dt)
    y_prompt, k_p, v_p, bconv_prompt, cconv_prompt, ffconv_prompt = trunk(
        x_prompt, p_prompt, None, z_b, z_c, z_f, True,
        w_in, w_o, sb_bias, b_conv_w, c_conv_w, c_conv_b, c_ln_g, c_ln_b,
        w_gate, w_up, w_down, ff_conv_w, w_ple, w_ple_gate, g_norm)
    t = x_prompt.shape[1]
    k_prompt = k_p.reshape(DEPTH, bp, t // PAGE_SIZE, PAGE_SIZE, N_HEADS_A, HEAD_DIM)
    v_prompt = v_p.reshape(DEPTH, bp, t // PAGE_SIZE, PAGE_SIZE, N_HEADS_A, HEAD_DIM)
    y_sample, k_sample, v_sample, bconv_sample, cconv_sample, ffconv_sample = trunk(
        x_sample, p_sample, (cache_k, cache_v, page_table), state_bconv, state_cconv, state_ffconv, False,
        w_in, w_o, sb_bias, b_conv_w, c_conv_w, c_conv_b, c_ln_g, c_ln_b,
        w_gate, w_up, w_down, ff_conv_w, w_ple, w_ple_gate, g_norm)
    return (y_prompt, y_sample, k_prompt, v_prompt, bconv_prompt, cconv_prompt, ffconv_prompt,
            k_sample, v_sample, bconv_sample, cconv_sample, ffconv_sample)
```

```python
import functools

import jax
import jax.numpy as jnp
from jax import lax
from jax.experimental import pallas as pl
from jax.experimental.pallas import tpu as pltpu

F32 = jnp.float32
BF16 = jnp.bfloat16

N_HEADS = 8
HEAD_DIM = 64
D_A = N_HEADS * HEAD_DIM
D_B = 256
D_C = 256
D_IN = 3 * D_A + 3 * D_B + 2 * D_C
B_CONV = 3
C_CONV = 31
FF_CONV = 3
PAGE = 128
EPS = 1e-6
HEADS_PER_LANE_TILE = 2
N_PAIRS = N_HEADS // HEADS_PER_LANE_TILE
LANES = 128
SUBLANES = 8
VMEM_LIMIT_BYTES = 56 * 1024 * 1024
SOFTPLUS_LINEAR_ABOVE = 60.0

PROMPT_TILE = 512
ATTN_Q_ROWS = 512
ATTN_KEYS = 256
FF_CHUNK = 1024
CONV_ROWS = 64
DECODE_PAGES = 8


def _round_up(a, m):
    return (a + m - 1) // m * m


def _params(n_axes):
    return pltpu.CompilerParams(dimension_semantics=("arbitrary",) * n_axes,
                                vmem_limit_bytes=VMEM_LIMIT_BYTES)


def _rms(x, g):
    return x * lax.rsqrt(jnp.mean(x * x, axis=-1, keepdims=True) + EPS) * g


def _softplus(z):
    return jnp.maximum(z, jnp.log(1.0 + jnp.exp(jnp.minimum(z, SOFTPLUS_LINEAR_ABOVE))))


def _rev_cumsum(sp, tri):
    hi = sp.astype(BF16)
    lo = (sp - hi.astype(F32)).astype(BF16)
    return (jnp.dot(hi, tri, preferred_element_type=F32)
            + jnp.dot(lo, tri, preferred_element_type=F32))


def _inproj_kernel(x_ref, g_ref, w_ref, bw_ref, cw_ref, cb_ref, lg_ref, lb_ref, bprev_ref, cprev_ref,
                   q_ref, kb_ref, vb_ref, k_ref, v_ref, obc_ref, bst_ref, cst_ref,
                   ubuf, gbuf, *, tm, shift, multi_tile):
    t = pl.program_id(1)
    nb = (B_CONV - 1) * shift
    nc = (C_CONV - 1) * shift
    pb = ubuf.shape[0] - tm
    pc = gbuf.shape[0] - tm

    @pl.when(t == 0)
    def _():
        ubuf[pb - nb:pb, :] = bprev_ref[0]
        gbuf[pc - nc:pc, :] = cprev_ref[0]

    h = _rms(x_ref[0], g_ref[0:1, :]).astype(BF16)

    def proj(lo, width):
        return jnp.dot(h, w_ref[:, lo:lo + width], preferred_element_type=F32)

    q = proj(0, D_A) * (HEAD_DIM ** -0.5)
    k = proj(D_A, D_A)
    v = proj(2 * D_A, D_A)
    k_ref[0] = k
    v_ref[0] = v
    for p in range(N_PAIRS):
        cols = slice(p * LANES, (p + 1) * LANES)
        q_ref[0, p] = q[:, cols].astype(BF16)
        kb_ref[0, p] = k[:, cols].astype(BF16)
        vb_ref[0, p] = v[:, cols].astype(BF16)

    o = 3 * D_A
    bg = proj(o, D_B)
    cg = proj(o + D_B, D_B)
    bx = proj(o + 2 * D_B, D_B)
    ca = proj(o + 3 * D_B, D_C)
    cgt = proj(o + 3 * D_B + D_C, D_C)

    ubuf[pb:pb + tm, :] = cg * bx
    yb = jnp.zeros((tm, D_B), F32)
    for j in range(B_CONV):
        yb = yb + bw_ref[j:j + 1, :] * ubuf[pl.ds(pb - nb + j * shift, tm), :]
    obc_ref[0, :, 0:D_B] = (bg * yb).astype(BF16)
    bst_ref[0] = ubuf[pb + tm - nb:pb + tm, :]

    gbuf[pc:pc + tm, :] = ca * jax.nn.sigmoid(cgt)
    rows = min(CONV_ROWS, tm)
    for c in range(tm // rows):
        acc = jnp.zeros((rows, D_C), F32)
        for j in range(C_CONV):
            acc = acc + cw_ref[j:j + 1, :] * gbuf[pl.ds(pc - nc + j * shift + c * rows, rows), :]
        yc = acc + cb_ref[...]
        mu = jnp.mean(yc, axis=-1, keepdims=True)
        d = yc - mu
        var = jnp.mean(d * d, axis=-1, keepdims=True)
        yn = d * lax.rsqrt(var + EPS) * lg_ref[...] + lb_ref[...]
        obc_ref[0, c * rows:(c + 1) * rows, D_B:D_B + D_C] = (yn * jax.nn.sigmoid(yn)).astype(BF16)
    cst_ref[0] = gbuf[pc + tm - nc:pc + tm, :]

    if multi_tile:
        ubuf[0:pb, :] = ubuf[tm:tm + pb, :]
        gbuf[0:pc, :] = gbuf[tm:tm + pc, :]


def _inproj(x, g, w_in, bw, cw, cb, lg, lb, bprev, cprev, *, tm, shift):
    bsz, t, d = x.shape
    nt = t // tm
    nb = (B_CONV - 1) * shift
    nc = (C_CONV - 1) * shift
    pb = _round_up(nb, SUBLANES)
    pc = _round_up(nc, SUBLANES)
    assert t % tm == 0 and (nt == 1 or tm >= pc)
    const = lambda *shape: pl.BlockSpec(shape, lambda b, i: (0,) * len(shape))
    pair_spec = pl.BlockSpec((1, N_PAIRS, tm, LANES), lambda b, i: (b, 0, i, 0))
    row_spec = pl.BlockSpec((1, tm, D_A), lambda b, i: (b, i, 0))
    pair_shape = jax.ShapeDtypeStruct((bsz, N_PAIRS, t, LANES), BF16)
    return pl.pallas_call(
        functools.partial(_inproj_kernel, tm=tm, shift=shift, multi_tile=nt > 1),
        grid=(bsz, nt),
        in_specs=[pl.BlockSpec((1, tm, d), lambda b, i: (b, i, 0)),
                  const(5, d), const(d, D_IN), const(B_CONV, D_B), const(C_CONV, D_C),
                  const(1, D_C), const(1, D_C), const(1, D_C),
                  pl.BlockSpec((1, nb, D_B), lambda b, i: (b, 0, 0)),
                  pl.BlockSpec((1, nc, D_C), lambda b, i: (b, 0, 0))],
        out_specs=[pair_spec, pair_spec, pair_spec, row_spec, row_spec, row_spec,
                   pl.BlockSpec((1, nb, D_B), lambda b, i: (b, 0, 0)),
                   pl.BlockSpec((1, nc, D_C), lambda b, i: (b, 0, 0))],
        out_shape=[pair_shape, pair_shape, pair_shape,
                   jax.ShapeDtypeStruct((bsz, t, D_A), F32), jax.ShapeDtypeStruct((bsz, t, D_A), F32),
                   jax.ShapeDtypeStruct((bsz, t, D_B + D_C), BF16),
                   jax.ShapeDtypeStruct((bsz, nb, D_B), F32), jax.ShapeDtypeStruct((bsz, nc, D_C), F32)],
        scratch_shapes=[pltpu.VMEM((pb + tm, D_B), F32), pltpu.VMEM((pc + tm, D_C), F32)],
        compiler_params=_params(2),
        name="inproj",
    )(x, g, w_in, bw, cw, cb, lg, lb, bprev, cprev)


def _prompt_attn_kernel(bias_ref, q_ref, k_ref, v_ref, tri_ref, o_ref, *, tq, tk):
    p = pl.program_id(1)
    qi = pl.program_id(2)
    tri = tri_ref[...]
    n_sub = tq // tk
    row = lax.broadcasted_iota(jnp.int32, (tq, tk), 0)
    col = lax.broadcasted_iota(jnp.int32, (tq, tk), 1)
    heads = []
    for hh in range(HEADS_PER_LANE_TILE):
        lanes = slice(hh * HEAD_DIM, (hh + 1) * HEAD_DIM)
        heads.append((lanes, bias_ref[p * HEADS_PER_LANE_TILE + hh], q_ref[0, 0, :, lanes]))

    def step(first_key, state, on_diagonal):
        new_state = []
        for (lanes, bias, qh), (carry, acc) in zip(heads, state):
            parts = []
            for s in reversed(range(n_sub)):
                start = pl.multiple_of(first_key + s * tk, tk)
                kh = k_ref[0, 0, pl.ds(start, tk), lanes]
                z = lax.dot_general(qh, kh, (((1,), (1,)), ((), ())), preferred_element_type=F32) + bias
                sp = _softplus(z)
                mask = (col + s * tk < row) if on_diagonal else None
                if on_diagonal:
                    sp = jnp.where(mask, sp, 0.0)
                parts.append((start, z, _rev_cumsum(sp, tri), mask))
            for start, z, within, mask in parts:
                a = jnp.exp(z - (within + carry))
                if on_diagonal:
                    a = jnp.where(mask, a, 0.0)
                vh = v_ref[0, 0, pl.ds(start, tk), lanes]
                acc = acc + jnp.dot(a.astype(BF16), vh, preferred_element_type=F32)
                carry = carry + within[:, 0:1]
            new_state.append((carry, acc))
        return tuple(new_state)

    zero = (jnp.zeros((tq, 1), F32), jnp.zeros((tq, HEAD_DIM), F32))
    state = step(qi * tq, (zero,) * HEADS_PER_LANE_TILE, True)
    state = lax.fori_loop(0, qi, lambda i, st: step((qi - 1 - i) * tq, st, False), state)
    o_ref[0] = jnp.concatenate([acc for _, acc in state], axis=1).astype(BF16)


def _prompt_attn(q, kb, vb, bias, tri, *, tq):
    bsz, _, t, _ = q.shape
    tk = tri.shape[0]
    assert t % tq == 0 and tq % tk == 0
    kv_spec = pl.BlockSpec((1, 1, t, LANES), lambda b, p, i: (b, p, 0, 0))
    return pl.pallas_call(
        functools.partial(_prompt_attn_kernel, tq=tq, tk=tk),
        grid=(bsz, N_PAIRS, t // tq),
        in_specs=[pl.BlockSpec(memory_space=pltpu.SMEM),
                  pl.BlockSpec((1, 1, tq, LANES), lambda b, p, i: (b, p, i, 0)),
                  kv_spec, kv_spec,
                  pl.BlockSpec((tk, tk), lambda b, p, i: (0, 0))],
        out_specs=pl.BlockSpec((1, tq, LANES), lambda b, p, i: (b, i, p)),
        out_shape=jax.ShapeDtypeStruct((bsz, t, D_A), BF16),
        compiler_params=_params(3),
        name="prompt_attn",
    )(bias, q, kb, vb, tri)


def _decode_attn_kernel(pt_ref, q_ref, bias_ref, tri_ref, bd_ref, kn_ref, vn_ref, *rest, n_fetch):
    k_refs = rest[:n_fetch]
    v_refs = rest[n_fetch:2 * n_fetch]
    o_ref, acc_ref, carry_ref = rest[2 * n_fetch:]
    g = pl.program_id(1)
    q = q_ref[0]
    bias = bias_ref[...]
    tri = tri_ref[...]
    n_rows = q.shape[0]

    def blocks(kv, carry, acc, mask):
        parts = []
        for kt, vt in kv:
            z = jnp.dot(q, kt, preferred_element_type=F32) + bias
            sp = _softplus(z)
            if mask is not None:
                sp = jnp.where(mask, sp, 0.0)
            parts.append((z, _rev_cumsum(sp, tri), vt))
        for z, within, vt in parts:
            a = jnp.exp(z - (within + carry))
            if mask is not None:
                a = jnp.where(mask, a, 0.0)
            acc = acc + lax.dot_general(a.astype(BF16), vt, (((1,), (1,)), ((), ())),
                                        preferred_element_type=F32)
            carry = carry + within[:, 0:1]
        return carry, acc

    @pl.when(g == 0)
    def _():
        row = lax.broadcasted_iota(jnp.int32, (n_rows, PAGE), 0)
        col = lax.broadcasted_iota(jnp.int32, (n_rows, PAGE), 1)
        mask = col * N_HEADS < row - row % N_HEADS
        carry, acc = blocks([(kn_ref[0], vn_ref[0])], jnp.zeros((n_rows, 1), F32),
                            jnp.zeros((n_rows, D_A), F32), mask)
        carry_ref[...] = carry
        acc_ref[...] = acc

    kv = [(k_refs[i][...].astype(BF16), v_refs[i][...].astype(BF16)) for i in range(n_fetch)]
    carry, acc = blocks(kv, carry_ref[...], acc_ref[...], None)
    carry_ref[...] = carry
    acc_ref[...] = acc

    @pl.when(g == pl.num_programs(1) - 1)
    def _():
        own_head = acc * bd_ref[...]
        n_q = n_rows // N_HEADS
        o_ref[0] = jnp.sum(own_head.reshape(n_q, N_HEADS, D_A), axis=1).astype(BF16)


def _decode_attn(page_table, qbd, bias_rows, tri, bdmask, knew_t, vnew_t, cache_kt, cache_vt, *, layer, n_fetch):
    n_seq, n_rows, _ = qbd.shape
    n_pages = page_table.shape[1]
    assert n_pages % n_fetch == 0 and n_rows % N_HEADS == 0
    n_q = n_rows // N_HEADS

    def page_spec(i):
        return pl.BlockSpec((None, None, D_A, PAGE),
                            lambda s, g, pt: (layer, pt[s, n_pages - 1 - (g * n_fetch + i)], 0, 0))

    const = lambda *shape: pl.BlockSpec(shape, lambda s, g, pt: (0,) * len(shape))
    seq3 = lambda *shape: pl.BlockSpec((1,) + shape, lambda s, g, pt: (s, 0, 0))
    grid_spec = pltpu.PrefetchScalarGridSpec(
        num_scalar_prefetch=1,
        grid=(n_seq, n_pages // n_fetch),
        in_specs=[seq3(n_rows, D_A), const(n_rows, PAGE), const(PAGE, PAGE), const(n_rows, D_A),
                  seq3(D_A, PAGE), seq3(D_A, PAGE)]
                 + [page_spec(i) for i in range(n_fetch)] * 2,
        out_specs=seq3(n_q, D_A),
        scratch_shapes=[pltpu.VMEM((n_rows, D_A), F32), pltpu.VMEM((n_rows, 1), F32)],
    )
    return pl.pallas_call(
        functools.partial(_decode_attn_kernel, n_fetch=n_fetch),
        grid_spec=grid_spec,
        out_shape=jax.ShapeDtypeStruct((n_seq, n_q, D_A), BF16),
        compiler_params=_params(2),
        name="decode_attn",
    )(page_table, qbd, bias_rows, tri, bdmask, knew_t, vnew_t,
      *([cache_kt] * n_fetch), *([cache_vt] * n_fetch))


def _post_kernel(x_ref, oa_ref, obc_ref, pe_ref, g_ref, wo_ref, wg_ref, wu_ref, wd_ref, fw_ref,
                 wple_ref, wpg_ref, fprev_ref,
                 y_ref, fst_ref,
                 x1_ref, h2_ref, acc_ref, fbuf, fcar, *, tm, shift, multi_tile):
    t = pl.program_id(1)
    f = pl.program_id(2)
    nf = (FF_CONV - 1) * shift
    pf = fbuf.shape[0] - tm

    @pl.when(f == 0)
    def _():
        m = (jnp.dot(oa_ref[0], wo_ref[0:D_A, :], preferred_element_type=F32)
             + jnp.dot(obc_ref[0], wo_ref[D_A:, :], preferred_element_type=F32))
        x1 = x_ref[0] + _rms(m, g_ref[1:2, :])
        x1_ref[...] = x1
        h2_ref[...] = _rms(x1, g_ref[2:3, :]).astype(BF16)
        acc_ref[...] = jnp.zeros_like(acc_ref)

    h2 = h2_ref[...]
    fbuf[pf:pf + tm, :] = jnp.dot(h2, wg_ref[...], preferred_element_type=F32)

    @pl.when(t == 0)
    def _():
        fbuf[pf - nf:pf, :] = fprev_ref[0]

    if multi_tile:
        @pl.when(t > 0)
        def _():
            fbuf[0:pf, :] = fcar[f]

    gt = jnp.zeros((tm, fbuf.shape[1]), F32)
    for j in range(FF_CONV):
        gt = gt + fw_ref[j:j + 1, :] * fbuf[pl.ds(pf - nf + j * shift, tm), :]
    fst_ref[0] = fbuf[pf + tm - nf:pf + tm, :]
    if multi_tile:
        fcar[f] = fbuf[tm:tm + pf, :]

    up = jnp.dot(h2, wu_ref[...], preferred_element_type=F32)
    act = (jax.nn.gelu(gt, approximate=True) * up).astype(BF16)
    acc_ref[...] += jnp.dot(act, wd_ref[...], preferred_element_type=F32)

    @pl.when(f == pl.num_programs(2) - 1)
    def _():
        x2 = x1_ref[...] + _rms(acc_ref[...], g_ref[3:4, :])
        e = _rms(jnp.dot(pe_ref[0].astype(BF16), wple_ref[...], preferred_element_type=F32), g_ref[4:5, :])
        gate = jax.nn.sigmoid(jnp.dot(x2.astype(BF16), wpg_ref[...], preferred_element_type=F32))
        y_ref[0] = x2 + e * gate


def _post(x, oa, obc, pe, g, wo, wg, wu, wd, fw, wple, wpg, fprev, *, tm, shift, fc):
    bsz, t, d = x.shape
    d_ff = wg.shape[1]
    d_ple = pe.shape[2]
    nt = t // tm
    n_chunks = d_ff // fc
    nf = (FF_CONV - 1) * shift
    pf = _round_up(nf, SUBLANES)
    assert t % tm == 0 and d_ff % fc == 0 and (nt == 1 or tm >= pf)
    const = lambda *shape: pl.BlockSpec(shape, lambda b, i, f: (0,) * len(shape))
    tile = lambda width: pl.BlockSpec((1, tm, width), lambda b, i, f: (b, i, 0))
    return pl.pallas_call(
        functools.partial(_post_kernel, tm=tm, shift=shift, multi_tile=nt > 1),
        grid=(bsz, nt, n_chunks),
        in_specs=[tile(d), tile(D_A), tile(D_B + D_C), tile(d_ple),
                  const(5, d), const(D_A + D_B + D_C, d),
                  pl.BlockSpec((d, fc), lambda b, i, f: (0, f)),
                  pl.BlockSpec((d, fc), lambda b, i, f: (0, f)),
                  pl.BlockSpec((fc, d), lambda b, i, f: (f, 0)),
                  pl.BlockSpec((FF_CONV, fc), lambda b, i, f: (0, f)),
                  const(d_ple, d), const(d, d),
                  pl.BlockSpec((1, nf, fc), lambda b, i, f: (b, 0, f))],
        out_specs=[tile(d), pl.BlockSpec((1, nf, fc), lambda b, i, f: (b, 0, jnp.where(i == nt - 1, f, 0)))],
        out_shape=[jax.ShapeDtypeStruct((bsz, t, d), F32), jax.ShapeDtypeStruct((bsz, nf, d_ff), F32)],
        scratch_shapes=[pltpu.VMEM((tm, d), F32), pltpu.VMEM((tm, d), BF16), pltpu.VMEM((tm, d), F32),
                        pltpu.VMEM((pf + tm, fc), F32), pltpu.VMEM((n_chunks, pf, fc), F32)],
        compiler_params=_params(3),
        name="post",
    )(x, oa, obc, pe, g, wo, wg, wu, wd, fw, wple, wpg, fprev)


def _tri(n):
    j = lax.broadcasted_iota(jnp.int32, (n, n), 0)
    s = lax.broadcasted_iota(jnp.int32, (n, n), 1)
    return (j >= s).astype(BF16)


def _layer_weights(i, w_in, w_o, w_gate, w_up, w_down, w_ple, w_ple_gate):
    return tuple(w[i].astype(BF16) for w in (w_in, w_o, w_gate, w_up, w_down, w_ple, w_ple_gate))


def _time_major(a):
    n_seq, steps, c = a.shape
    return a.transpose(1, 0, 2).reshape(1, steps * n_seq, c)


def _seq_major(a, n_seq):
    _, rows, c = a.shape
    return a.reshape(rows // n_seq, n_seq, c).transpose(1, 0, 2)


def _pairs_to_seq(a, n_seq):
    steps = a.shape[2] // n_seq
    return a.reshape(N_PAIRS, steps, n_seq, LANES).transpose(2, 1, 0, 3).reshape(n_seq, steps, D_A)


def kernel(x_prompt, x_sample, p_prompt, p_sample, cache_k, cache_v, page_table, state_bconv, state_cconv, state_ffconv, w_in, w_o, sb_bias, b_conv_w, c_conv_w, c_conv_b, c_ln_g, c_ln_b, w_gate, w_up, w_down, ff_conv_w, w_ple, w_ple_gate, g_norm):
    depth = w_in.shape[0]
    bp, t, d = x_prompt.shape
    n_seq, steps, _ = x_sample.shape
    d_ff = w_gate.shape[2]
    n_pool = cache_k.shape[1]

    cache_kt = cache_k.transpose(0, 1, 3, 4, 2).reshape(depth, n_pool, D_A, PAGE)
    cache_vt = cache_v.transpose(0, 1, 3, 4, 2).reshape(depth, n_pool, D_A, PAGE)

    tri_prompt = _tri(ATTN_KEYS)
    tri_page = _tri(PAGE)
    n_rows = steps * N_HEADS
    r = lax.broadcasted_iota(jnp.int32, (n_rows, D_A), 0)
    c = lax.broadcasted_iota(jnp.int32, (n_rows, D_A), 1)
    bdmask = ((r % N_HEADS) == (c // HEAD_DIM)).astype(F32)

    xp = x_prompt
    xs = _time_major(x_sample)
    zeros = lambda rows, width: jnp.zeros((bp, rows, width), F32)
    outs = {name: [] for name in ("kp", "vp", "bp", "cp", "fp", "ks", "vs", "bs", "cs", "fs")}

    for i in range(depth):
        wi, wo, wg, wu, wd, wple, wpg = _layer_weights(i, w_in, w_o, w_gate, w_up, w_down, w_ple, w_ple_gate)
        conv_args = (b_conv_w[i], c_conv_w[i], c_conv_b[i][None], c_ln_g[i][None], c_ln_b[i][None])

        q, kb, vb, k, v, obc, bst, cst = _inproj(
            xp, g_norm[i], wi, *conv_args, zeros(B_CONV - 1, D_B), zeros(C_CONV - 1, D_C),
            tm=PROMPT_TILE, shift=1)
        oa = _prompt_attn(q, kb, vb, sb_bias[i], tri_prompt, tq=ATTN_Q_ROWS)
        xp, fst = _post(xp, oa, obc, p_prompt[i], g_norm[i], wo, wg, wu, wd, ff_conv_w[i], wple, wpg,
                        zeros(FF_CONV - 1, d_ff), tm=PROMPT_TILE, shift=1, fc=FF_CHUNK)
        outs["kp"].append(k.reshape(bp, t // PAGE, PAGE, N_HEADS, HEAD_DIM))
        outs["vp"].append(v.reshape(bp, t // PAGE, PAGE, N_HEADS, HEAD_DIM))
        outs["bp"].append(bst)
        outs["cp"].append(cst)
        outs["fp"].append(fst)

        q, kb, vb, k, v, obc, bst, cst = _inproj(
            xs, g_norm[i], wi, *conv_args, _time_major(state_bconv[i]), _time_major(state_cconv[i]),
            tm=steps * n_seq, shift=n_seq)
        q_seq = _pairs_to_seq(q, n_seq)
        qbd = (jnp.repeat(q_seq, N_HEADS, axis=1) * bdmask.astype(BF16))
        pad = ((0, 0), (0, 0), (0, PAGE - steps))
        knew_t = jnp.pad(_pairs_to_seq(kb, n_seq).transpose(0, 2, 1), pad)
        vnew_t = jnp.pad(_pairs_to_seq(vb, n_seq).transpose(0, 2, 1), pad)
        bias_rows = jnp.broadcast_to(jnp.tile(sb_bias[i], steps)[:, None], (n_rows, PAGE))
        oa = _decode_attn(page_table, qbd, bias_rows, tri_page, bdmask, knew_t, vnew_t, cache_kt, cache_vt,
                          layer=i, n_fetch=DECODE_PAGES)
        xs, fst = _post(xs, _time_major(oa), obc, _time_major(p_sample[i]), g_norm[i], wo, wg, wu, wd,
                        ff_conv_w[i], wple, wpg, _time_major(state_ffconv[i]),
                        tm=steps * n_seq, shift=n_seq, fc=FF_CHUNK)
        outs["ks"].append(_seq_major(k, n_seq).reshape(n_seq, steps, N_HEADS, HEAD_DIM))
        outs["vs"].append(_seq_major(v, n_seq).reshape(n_seq, steps, N_HEADS, HEAD_DIM))
        outs["bs"].append(_seq_major(bst, n_seq))
        outs["cs"].append(_seq_major(cst, n_seq))
        outs["fs"].append(_seq_major(fst, n_seq))

    st = {name: jnp.stack(vals) for name, vals in outs.items()}
    return (xp, _seq_major(xs, n_seq), st["kp"], st["vp"], st["bp"], st["cp"], st["fp"],
            st["ks"], st["vs"], st["bs"], st["cs"], st["fs"])
```

```python
import functools

import jax
import jax.numpy as jnp
from jax import lax
from jax.experimental import pallas as pl
from jax.experimental.pallas import tpu as pltpu

F32 = jnp.float32
BF16 = jnp.bfloat16

N_HEADS = 8
HEAD_DIM = 64
D_A = N_HEADS * HEAD_DIM
D_B = 256
D_C = 256
D_IN = 3 * D_A + 3 * D_B + 2 * D_C
B_CONV = 3
C_CONV = 31
FF_CONV = 3
PAGE = 128
EPS = 1e-6
HEADS_PER_LANE_TILE = 2
N_PAIRS = N_HEADS // HEADS_PER_LANE_TILE
LANES = 128
SUBLANES = 8
VMEM_LIMIT_BYTES = 56 * 1024 * 1024
LOG2_E = 1.4426950408889634
SCORE_LOG2_MAX = 126.0

PROMPT_TILE = 512
ATTN_Q_ROWS = 512
ATTN_KEYS = 256
FF_CHUNK = 1024
CONV_ROWS = 64
DECODE_PAGES = 16


def _round_up(a, m):
    return (a + m - 1) // m * m


def _params(n_axes):
    return pltpu.CompilerParams(dimension_semantics=("arbitrary",) * n_axes,
                                vmem_limit_bytes=VMEM_LIMIT_BYTES)


def _rms(x, g):
    return x * lax.rsqrt(jnp.mean(x * x, axis=-1, keepdims=True) + EPS) * g


def _score_terms(s2, bias2):
    z2 = jnp.minimum(s2 + bias2, SCORE_LOG2_MAX)
    return z2, jnp.log(1.0 + jnp.exp2(z2))


def _rev_cumsum(sp, tri):
    return jnp.dot(sp.astype(BF16), tri, preferred_element_type=F32)


def _stick_weights(z2, within, carry):
    return jnp.exp2(z2 - (within + carry) * LOG2_E)


def _inproj_kernel(x_ref, g_ref, w_ref, bw_ref, cw_ref, cb_ref, lg_ref, lb_ref, bprev_ref, cprev_ref,
                   q_ref, kb_ref, vb_ref, kt_ref, vt_ref, obc_ref, bst_ref, cst_ref,
                   ubuf, gbuf, *, tm, shift, multi_tile):
    t = pl.program_id(1)
    nb = (B_CONV - 1) * shift
    nc = (C_CONV - 1) * shift
    pb = ubuf.shape[0] - tm
    pc = gbuf.shape[0] - tm

    @pl.when(t == 0)
    def _():
        ubuf[pb - nb:pb, :] = bprev_ref[0]
        gbuf[pc - nc:pc, :] = cprev_ref[0]

    h = _rms(x_ref[0], g_ref[0:1, :]).astype(BF16)

    def proj(lo, width):
        return jnp.dot(h, w_ref[:, lo:lo + width], preferred_element_type=F32)

    q = proj(0, D_A) * (HEAD_DIM ** -0.5 * LOG2_E)
    k = proj(D_A, D_A)
    v = proj(2 * D_A, D_A)
    for pg in range(tm // PAGE):
        rows = slice(pg * PAGE, (pg + 1) * PAGE)
        kt_ref[0, pg] = k[rows, :].T
        vt_ref[0, pg] = v[rows, :].T
    for p in range(N_PAIRS):
        cols = slice(p * LANES, (p + 1) * LANES)
        q_ref[0, p] = q[:, cols].astype(BF16)
        kb_ref[0, p] = k[:, cols].astype(BF16)
        vb_ref[0, p] = v[:, cols].astype(BF16)

    o = 3 * D_A
    bg = proj(o, D_B)
    cg = proj(o + D_B, D_B)
    bx = proj(o + 2 * D_B, D_B)
    ca = proj(o + 3 * D_B, D_C)
    cgt = proj(o + 3 * D_B + D_C, D_C)

    ubuf[pb:pb + tm, :] = cg * bx
    yb = jnp.zeros((tm, D_B), F32)
    for j in range(B_CONV):
        yb = yb + bw_ref[j:j + 1, :] * ubuf[pl.ds(pb - nb + j * shift, tm), :]
    obc_ref[0, :, 0:D_B] = (bg * yb).astype(BF16)
    bst_ref[0] = ubuf[pb + tm - nb:pb + tm, :]

    gbuf[pc:pc + tm, :] = ca * jax.nn.sigmoid(cgt)
    rows = min(CONV_ROWS, tm)
    for c in range(tm // rows):
        acc = jnp.zeros((rows, D_C), F32)
        for j in range(C_CONV):
            acc = acc + cw_ref[j:j + 1, :] * gbuf[pl.ds(pc - nc + j * shift + c * rows, rows), :]
        yc = acc + cb_ref[...]
        mu = jnp.mean(yc, axis=-1, keepdims=True)
        d = yc - mu
        var = jnp.mean(d * d, axis=-1, keepdims=True)
        yn = d * lax.rsqrt(var + EPS) * lg_ref[...] + lb_ref[...]
        obc_ref[0, c * rows:(c + 1) * rows, D_B:D_B + D_C] = (yn * jax.nn.sigmoid(yn)).astype(BF16)
    cst_ref[0] = gbuf[pc + tm - nc:pc + tm, :]

    if multi_tile:
        ubuf[0:pb, :] = ubuf[tm:tm + pb, :]
        gbuf[0:pc, :] = gbuf[tm:tm + pc, :]


def _inproj(x, g, w_in, bw, cw, cb, lg, lb, bprev, cprev, *, tm, shift):
    bsz, t, d = x.shape
    nt = t // tm
    assert tm % PAGE == 0
    nb = (B_CONV - 1) * shift
    nc = (C_CONV - 1) * shift
    pb = _round_up(nb, SUBLANES)
    pc = _round_up(nc, SUBLANES)
    assert t % tm == 0 and (nt == 1 or tm >= pc)
    const = lambda *shape: pl.BlockSpec(shape, lambda b, i: (0,) * len(shape))
    pair_spec = pl.BlockSpec((1, N_PAIRS, tm, LANES), lambda b, i: (b, 0, i, 0))
    row_spec = pl.BlockSpec((1, tm, D_A), lambda b, i: (b, i, 0))
    page_spec = pl.BlockSpec((1, tm // PAGE, D_A, PAGE), lambda b, i: (b, i, 0, 0))
    page_shape = jax.ShapeDtypeStruct((bsz, t // PAGE, D_A, PAGE), F32)
    pair_shape = jax.ShapeDtypeStruct((bsz, N_PAIRS, t, LANES), BF16)
    return pl.pallas_call(
        functools.partial(_inproj_kernel, tm=tm, shift=shift, multi_tile=nt > 1),
        grid=(bsz, nt),
        in_specs=[pl.BlockSpec((1, tm, d), lambda b, i: (b, i, 0)),
                  const(5, d), const(d, D_IN), const(B_CONV, D_B), const(C_CONV, D_C),
                  const(1, D_C), const(1, D_C), const(1, D_C),
                  pl.BlockSpec((1, nb, D_B), lambda b, i: (b, 0, 0)),
                  pl.BlockSpec((1, nc, D_C), lambda b, i: (b, 0, 0))],
        out_specs=[pair_spec, pair_spec, pair_spec, page_spec, page_spec, row_spec,
                   pl.BlockSpec((1, nb, D_B), lambda b, i: (b, 0, 0)),
                   pl.BlockSpec((1, nc, D_C), lambda b, i: (b, 0, 0))],
        out_shape=[pair_shape, pair_shape, pair_shape, page_shape, page_shape,
                   jax.ShapeDtypeStruct((bsz, t, D_B + D_C), BF16),
                   jax.ShapeDtypeStruct((bsz, nb, D_B), F32), jax.ShapeDtypeStruct((bsz, nc, D_C), F32)],
        scratch_shapes=[pltpu.VMEM((pb + tm, D_B), F32), pltpu.VMEM((pc + tm, D_C), F32)],
        compiler_params=_params(2),
        name="inproj",
    )(x, g, w_in, bw, cw, cb, lg, lb, bprev, cprev)


def _prompt_attn_kernel(bias_ref, q_ref, k_ref, v_ref, tri_ref, o_ref, *, tq, tk):
    p = pl.program_id(1)
    qi = pl.program_id(2)
    tri = tri_ref[...]
    n_sub = tq // tk
    row = lax.broadcasted_iota(jnp.int32, (tq, tk), 0)
    col = lax.broadcasted_iota(jnp.int32, (tq, tk), 1)
    heads = []
    for hh in range(HEADS_PER_LANE_TILE):
        lanes = slice(hh * HEAD_DIM, (hh + 1) * HEAD_DIM)
        heads.append((lanes, bias_ref[p * HEADS_PER_LANE_TILE + hh], q_ref[0, 0, :, lanes]))

    def step(first_key, state, on_diagonal):
        new_state = []
        for (lanes, bias, qh), (carry, acc) in zip(heads, state):
            parts = []
            for s in reversed(range(n_sub)):
                start = pl.multiple_of(first_key + s * tk, tk)
                kh = k_ref[0, 0, pl.ds(start, tk), lanes]
                s2 = lax.dot_general(qh, kh, (((1,), (1,)), ((), ())), preferred_element_type=F32)
                z, sp = _score_terms(s2, bias)
                mask = (col + s * tk < row) if on_diagonal else None
                if on_diagonal:
                    sp = jnp.where(mask, sp, 0.0)
                parts.append((start, z, _rev_cumsum(sp, tri), mask))
            for start, z, within, mask in parts:
                a = _stick_weights(z, within, carry)
                if on_diagonal:
                    a = jnp.where(mask, a, 0.0)
                vh = v_ref[0, 0, pl.ds(start, tk), lanes]
                acc = acc + jnp.dot(a.astype(BF16), vh, preferred_element_type=F32)
                carry = carry + within[:, 0:1]
            new_state.append((carry, acc))
        return tuple(new_state)

    zero = (jnp.zeros((tq, 1), F32), jnp.zeros((tq, HEAD_DIM), F32))
    state = step(qi * tq, (zero,) * HEADS_PER_LANE_TILE, True)
    state = lax.fori_loop(0, qi, lambda i, st: step((qi - 1 - i) * tq, st, False), state)
    o_ref[0] = jnp.concatenate([acc for _, acc in state], axis=1).astype(BF16)


def _prompt_attn(q, kb, vb, bias, tri, *, tq):
    bsz, _, t, _ = q.shape
    tk = tri.shape[0]
    assert t % tq == 0 and tq % tk == 0
    kv_spec = pl.BlockSpec((1, 1, t, LANES), lambda b, p, i: (b, p, 0, 0))
    return pl.pallas_call(
        functools.partial(_prompt_attn_kernel, tq=tq, tk=tk),
        grid=(bsz, N_PAIRS, t // tq),
        in_specs=[pl.BlockSpec(memory_space=pltpu.SMEM),
                  pl.BlockSpec((1, 1, tq, LANES), lambda b, p, i: (b, p, i, 0)),
                  kv_spec, kv_spec,
                  pl.BlockSpec((tk, tk), lambda b, p, i: (0, 0))],
        out_specs=pl.BlockSpec((1, tq, LANES), lambda b, p, i: (b, i, p)),
        out_shape=jax.ShapeDtypeStruct((bsz, t, D_A), BF16),
        compiler_params=_params(3),
        name="prompt_attn",
    )(bias, q, kb, vb, tri)


def _decode_attn_kernel(pt_ref, q_ref, bias_ref, tri_ref, bd_ref, kn_ref, vn_ref, *rest, n_fetch):
    k_refs = rest[:n_fetch]
    v_refs = rest[n_fetch:2 * n_fetch]
    o_ref, acc_ref, carry_ref = rest[2 * n_fetch:]
    g = pl.program_id(1)
    q = q_ref[0]
    tri = tri_ref[...]
    n_rows = q.shape[0]

    def blocks(kv, carry, mask):
        n = len(kv)
        s2 = jnp.concatenate([jnp.dot(q, kt, preferred_element_type=F32) for kt, _ in kv], axis=0)
        z, sp = _score_terms(s2, bias_ref[0:n * n_rows, :])
        if mask is not None:
            sp = jnp.where(mask, sp, 0.0)
        within = _rev_cumsum(sp, tri)
        carries = []
        for i in range(n):
            carries.append(carry)
            carry = carry + within[i * n_rows:(i + 1) * n_rows, 0:1]
        a = _stick_weights(z, within, jnp.concatenate(carries, axis=0))
        if mask is not None:
            a = jnp.where(mask, a, 0.0)
        a = a.astype(BF16)
        out_t = None
        for i, (_, vt) in enumerate(kv):
            part = lax.dot_general(vt, a[i * n_rows:(i + 1) * n_rows], (((1,), (1,)), ((), ())),
                                   preferred_element_type=F32)
            out_t = part if out_t is None else out_t + part
        return carry, out_t

    @pl.when(g == 0)
    def _():
        row = lax.broadcasted_iota(jnp.int32, (n_rows, PAGE), 0)
        col = lax.broadcasted_iota(jnp.int32, (n_rows, PAGE), 1)
        mask = col * N_HEADS < row - row % N_HEADS
        carry, out_t = blocks([(kn_ref[0], vn_ref[0])], jnp.zeros((n_rows, 1), F32), mask)
        carry_ref[...] = carry
        acc_ref[...] = jnp.zeros_like(acc_ref)
        acc_ref[:, 0:n_rows] = out_t

    kv = [(k_refs[i][...].astype(BF16), v_refs[i][...].astype(BF16)) for i in range(n_fetch)]
    carry, out_t = blocks(kv, carry_ref[...], None)
    carry_ref[...] = carry
    acc_ref[:, 0:n_rows] += out_t

    @pl.when(g == pl.num_programs(1) - 1)
    def _():
        own_head = acc_ref[...].T[0:n_rows, :] * bd_ref[...]
        n_q = n_rows // N_HEADS
        o_ref[0] = jnp.sum(own_head.reshape(n_q, N_HEADS, D_A), axis=1).astype(BF16)


def _decode_attn(page_table, qbd, bias_rows, tri, bdmask, knew_t, vnew_t, cache_kt, cache_vt, *, layer, n_fetch):
    n_seq, n_rows, _ = qbd.shape
    n_pages = page_table.shape[1]
    assert n_pages % n_fetch == 0 and n_rows % N_HEADS == 0
    n_q = n_rows // N_HEADS

    def page_spec(i):
        return pl.BlockSpec((None, None, D_A, PAGE),
                            lambda s, g, pt: (layer, pt[s, n_pages - 1 - (g * n_fetch + i)], 0, 0))

    const = lambda *shape: pl.BlockSpec(shape, lambda s, g, pt: (0,) * len(shape))
    seq3 = lambda *shape: pl.BlockSpec((1,) + shape, lambda s, g, pt: (s, 0, 0))
    grid_spec = pltpu.PrefetchScalarGridSpec(
        num_scalar_prefetch=1,
        grid=(n_seq, n_pages // n_fetch),
        in_specs=[seq3(n_rows, D_A), const(n_fetch * n_rows, PAGE), const(PAGE, PAGE), const(n_rows, D_A),
                  seq3(D_A, PAGE), seq3(D_A, PAGE)]
                 + [page_spec(i) for i in range(n_fetch)] * 2,
        out_specs=seq3(n_q, D_A),
        scratch_shapes=[pltpu.VMEM((D_A, LANES), F32), pltpu.VMEM((n_rows, 1), F32)],
    )
    return pl.pallas_call(
        functools.partial(_decode_attn_kernel, n_fetch=n_fetch),
        grid_spec=grid_spec,
        out_shape=jax.ShapeDtypeStruct((n_seq, n_q, D_A), BF16),
        compiler_params=_params(2),
        name="decode_attn",
    )(page_table, qbd, bias_rows, tri, bdmask, knew_t, vnew_t,
      *([cache_kt] * n_fetch), *([cache_vt] * n_fetch))


def _post_kernel(x_ref, oa_ref, obc_ref, pe_ref, g_ref, wo_ref, wg_ref, wu_ref, wd_ref, fw_ref,
                 wple_ref, wpg_ref, fprev_ref,
                 y_ref, fst_ref,
                 x1_ref, h2_ref, acc_ref, fbuf, fcar, *, tm, shift, multi_tile):
    t = pl.program_id(1)
    f = pl.program_id(2)
    nf = (FF_CONV - 1) * shift
    pf = fbuf.shape[0] - tm

    @pl.when(f == 0)
    def _():
        m = (jnp.dot(oa_ref[0], wo_ref[0:D_A, :], preferred_element_type=F32)
             + jnp.dot(obc_ref[0], wo_ref[D_A:, :], preferred_element_type=F32))
        x1 = x_ref[0] + _rms(m, g_ref[1:2, :])
        x1_ref[...] = x1
        h2_ref[...] = _rms(x1, g_ref[2:3, :]).astype(BF16)
        acc_ref[...] = jnp.zeros_like(acc_ref)

    h2 = h2_ref[...]
    fbuf[pf:pf + tm, :] = jnp.dot(h2, wg_ref[...], preferred_element_type=F32)

    @pl.when(t == 0)
    def _():
        fbuf[pf - nf:pf, :] = fprev_ref[0]

    if multi_tile:
        @pl.when(t > 0)
        def _():
            fbuf[0:pf, :] = fcar[f]

    gt = jnp.zeros((tm, fbuf.shape[1]), F32)
    for j in range(FF_CONV):
        gt = gt + fw_ref[j:j + 1, :] * fbuf[pl.ds(pf - nf + j * shift, tm), :]
    fst_ref[0] = fbuf[pf + tm - nf:pf + tm, :]
    if multi_tile:
        fcar[f] = fbuf[tm:tm + pf, :]

    up = jnp.dot(h2, wu_ref[...], preferred_element_type=F32)
    act = (jax.nn.gelu(gt, approximate=True) * up).astype(BF16)
    acc_ref[...] += jnp.dot(act, wd_ref[...], preferred_element_type=F32)

    @pl.when(f == pl.num_programs(2) - 1)
    def _():
        x2 = x1_ref[...] + _rms(acc_ref[...], g_ref[3:4, :])
        e = _rms(jnp.dot(pe_ref[0].astype(BF16), wple_ref[...], preferred_element_type=F32), g_ref[4:5, :])
        gate = jax.nn.sigmoid(jnp.dot(x2.astype(BF16), wpg_ref[...], preferred_element_type=F32))
        y_ref[0] = x2 + e * gate


def _post(x, oa, obc, pe, g, wo, wg, wu, wd, fw, wple, wpg, fprev, *, tm, shift, fc):
    bsz, t, d = x.shape
    d_ff = wg.shape[1]
    d_ple = pe.shape[2]
    nt = t // tm
    n_chunks = d_ff // fc
    nf = (FF_CONV - 1) * shift
    pf = _round_up(nf, SUBLANES)
    assert t % tm == 0 and d_ff % fc == 0 and (nt == 1 or tm >= pf)
    const = lambda *shape: pl.BlockSpec(shape, lambda b, i, f: (0,) * len(shape))
    tile = lambda width: pl.BlockSpec((1, tm, width), lambda b, i, f: (b, i, 0))
    return pl.pallas_call(
        functools.partial(_post_kernel, tm=tm, shift=shift, multi_tile=nt > 1),
        grid=(bsz, nt, n_chunks),
        in_specs=[tile(d), tile(D_A), tile(D_B + D_C), tile(d_ple),
                  const(5, d), const(D_A + D_B + D_C, d),
                  pl.BlockSpec((d, fc), lambda b, i, f: (0, f)),
                  pl.BlockSpec((d, fc), lambda b, i, f: (0, f)),
                  pl.BlockSpec((fc, d), lambda b, i, f: (f, 0)),
                  pl.BlockSpec((FF_CONV, fc), lambda b, i, f: (0, f)),
                  const(d_ple, d), const(d, d),
                  pl.BlockSpec((1, nf, fc), lambda b, i, f: (b, 0, f))],
        out_specs=[tile(d), pl.BlockSpec((1, nf, fc), lambda b, i, f: (b, 0, jnp.where(i == nt - 1, f, 0)))],
        out_shape=[jax.ShapeDtypeStruct((bsz, t, d), F32), jax.ShapeDtypeStruct((bsz, nf, d_ff), F32)],
        scratch_shapes=[pltpu.VMEM((tm, d), F32), pltpu.VMEM((tm, d), BF16), pltpu.VMEM((tm, d), F32),
                        pltpu.VMEM((pf + tm, fc), F32), pltpu.VMEM((n_chunks, pf, fc), F32)],
        compiler_params=_params(3),
        name="post",
    )(x, oa, obc, pe, g, wo, wg, wu, wd, fw, wple, wpg, fprev)


def _tri(n):
    j = lax.broadcasted_iota(jnp.int32, (n, n), 0)
    s = lax.broadcasted_iota(jnp.int32, (n, n), 1)
    return (j >= s).astype(BF16)


def _layer_weights(i, w_in, w_o, w_gate, w_up, w_down, w_ple, w_ple_gate):
    return tuple(w[i].astype(BF16) for w in (w_in, w_o, w_gate, w_up, w_down, w_ple, w_ple_gate))


def _time_major(a):
    n_seq, steps, c = a.shape
    return a.transpose(1, 0, 2).reshape(1, steps * n_seq, c)


def _seq_major(a, n_seq):
    _, rows, c = a.shape
    return a.reshape(rows // n_seq, n_seq, c).transpose(1, 0, 2)


def _pairs_to_seq(a, n_seq):
    steps = a.shape[2] // n_seq
    return a.reshape(N_PAIRS, steps, n_seq, LANES).transpose(2, 1, 0, 3).reshape(n_seq, steps, D_A)


def kernel(x_prompt, x_sample, p_prompt, p_sample, cache_k, cache_v, page_table, state_bconv, state_cconv, state_ffconv, w_in, w_o, sb_bias, b_conv_w, c_conv_w, c_conv_b, c_ln_g, c_ln_b, w_gate, w_up, w_down, ff_conv_w, w_ple, w_ple_gate, g_norm):
    depth = w_in.shape[0]
    bp, t, d = x_prompt.shape
    n_seq, steps, _ = x_sample.shape
    d_ff = w_gate.shape[2]
    n_pool = cache_k.shape[1]

    cache_kt = cache_k.transpose(0, 1, 3, 4, 2).reshape(depth, n_pool, D_A, PAGE)
    cache_vt = cache_v.transpose(0, 1, 3, 4, 2).reshape(depth, n_pool, D_A, PAGE)

    tri_prompt = _tri(ATTN_KEYS)
    tri_page = _tri(PAGE)
    n_rows = steps * N_HEADS
    r = lax.broadcasted_iota(jnp.int32, (n_rows, D_A), 0)
    c = lax.broadcasted_iota(jnp.int32, (n_rows, D_A), 1)
    bdmask = ((r % N_HEADS) == (c // HEAD_DIM)).astype(F32)

    xp = x_prompt
    xs = _time_major(x_sample)
    zeros = lambda rows, width: jnp.zeros((bp, rows, width), F32)
    outs = {name: [] for name in ("kp", "vp", "bp", "cp", "fp", "ks", "vs", "bs", "cs", "fs")}
    pages_out = lambda a: a.reshape(bp, t // PAGE, N_HEADS, HEAD_DIM, PAGE).transpose(0, 1, 4, 2, 3)

    for i in range(depth):
        wi, wo, wg, wu, wd, wple, wpg = _layer_weights(i, w_in, w_o, w_gate, w_up, w_down, w_ple, w_ple_gate)
        conv_args = (b_conv_w[i], c_conv_w[i], c_conv_b[i][None], c_ln_g[i][None], c_ln_b[i][None])
        bias2 = sb_bias[i] * LOG2_E

        q, kb, vb, kt, vt, obc, bst, cst = _inproj(
            xp, g_norm[i], wi, *conv_args, zeros(B_CONV - 1, D_B), zeros(C_CONV - 1, D_C),
            tm=PROMPT_TILE, shift=1)
        oa = _prompt_attn(q, kb, vb, bias2, tri_prompt, tq=ATTN_Q_ROWS)
        xp, fst = _post(xp, oa, obc, p_prompt[i], g_norm[i], wo, wg, wu, wd, ff_conv_w[i], wple, wpg,
                        zeros(FF_CONV - 1, d_ff), tm=PROMPT_TILE, shift=1, fc=FF_CHUNK)
        outs["kp"].append(pages_out(kt))
        outs["vp"].append(pages_out(vt))
        outs["bp"].append(bst)
        outs["cp"].append(cst)
        outs["fp"].append(fst)

        q, kb, vb, kt, vt, obc, bst, cst = _inproj(
            xs, g_norm[i], wi, *conv_args, _time_major(state_bconv[i]), _time_major(state_cconv[i]),
            tm=steps * n_seq, shift=n_seq)
        q_seq = _pairs_to_seq(q, n_seq)
        qbd = (jnp.repeat(q_seq, N_HEADS, axis=1) * bdmask.astype(BF16))
        pad = ((0, 0), (0, 0), (0, PAGE - steps))
        knew_t = jnp.pad(_pairs_to_seq(kb, n_seq).transpose(0, 2, 1), pad)
        vnew_t = jnp.pad(_pairs_to_seq(vb, n_seq).transpose(0, 2, 1), pad)
        bias_rows = jnp.broadcast_to(jnp.tile(bias2, steps * DECODE_PAGES)[:, None], (DECODE_PAGES * n_rows, PAGE))
        oa = _decode_attn(page_table, qbd, bias_rows, tri_page, bdmask, knew_t, vnew_t, cache_kt, cache_vt,
                          layer=i, n_fetch=DECODE_PAGES)
        xs, fst = _post(xs, _time_major(oa), obc, _time_major(p_sample[i]), g_norm[i], wo, wg, wu, wd,
                        ff_conv_w[i], wple, wpg, _time_major(state_ffconv[i]),
                        tm=steps * n_seq, shift=n_seq, fc=FF_CHUNK)
        outs["ks"].append(_seq_major(kt[:, 0].transpose(0, 2, 1), n_seq).reshape(n_seq, steps, N_HEADS, HEAD_DIM))
        outs["vs"].append(_seq_major(vt[:, 0].transpose(0, 2, 1), n_seq).reshape(n_seq, steps, N_HEADS, HEAD_DIM))
        outs["bs"].append(_seq_major(bst, n_seq))
        outs["cs"].append(_seq_major(cst, n_seq))
        outs["fs"].append(_seq_major(fst, n_seq))

    st = {name: jnp.stack(vals) for name, vals in outs.items()}
    return (xp, _seq_major(xs, n_seq), st["kp"], st["vp"], st["bp"], st["cp"], st["fp"],
            st["ks"], st["vs"], st["bs"], st["cs"], st["fs"])
```

```python
import functools

import jax
import jax.numpy as jnp
from jax import lax
from jax.experimental import pallas as pl
from jax.experimental.pallas import tpu as pltpu

F32 = jnp.float32
BF16 = jnp.bfloat16

N_HEADS = 8
HEAD_DIM = 64
D_A = N_HEADS * HEAD_DIM
D_B = 256
D_C = 256
D_IN = 3 * D_A + 3 * D_B + 2 * D_C
B_CONV = 3
C_CONV = 31
FF_CONV = 3
PAGE = 128
EPS = 1e-6
HEADS_PER_LANE_TILE = 2
N_PAIRS = N_HEADS // HEADS_PER_LANE_TILE
LANES = 128
SUBLANES = 8
VMEM_LIMIT_BYTES = 56 * 1024 * 1024
LOG2_E = 1.4426950408889634
BIAS_TERMS = 3
SCORE_LOG2_MAX = 126.0

PROMPT_TILE = 512
ATTN_Q_ROWS = 512
ATTN_KEYS = 256
FF_CHUNK = 1024
CONV_ROWS = 64
DECODE_PAGES = 16


def _round_up(a, m):
    return (a + m - 1) // m * m


def _params(n_axes):
    return pltpu.CompilerParams(dimension_semantics=("arbitrary",) * n_axes,
                                vmem_limit_bytes=VMEM_LIMIT_BYTES)


def _rms(x, g):
    return x * lax.rsqrt(jnp.mean(x * x, axis=-1, keepdims=True) + EPS) * g


def _score_terms(logit2):
    z2 = jnp.minimum(logit2, SCORE_LOG2_MAX)
    return z2, jnp.log(1.0 + jnp.exp2(z2))


def _rev_cumsum(sp, tri):
    return jnp.dot(sp.astype(BF16), tri, preferred_element_type=F32)


def _stick_weights(z2, within, carry):
    return jnp.exp2(z2 - (within + carry) * LOG2_E)


def _inproj_kernel(x_ref, g_ref, w_ref, bw_ref, cw_ref, cb_ref, lg_ref, lb_ref, bprev_ref, cprev_ref,
                   q_ref, kb_ref, vb_ref, kt_ref, vt_ref, obc_ref, bst_ref, cst_ref,
                   ubuf, gbuf, *, tm, shift, multi_tile):
    t = pl.program_id(1)
    nb = (B_CONV - 1) * shift
    nc = (C_CONV - 1) * shift
    pb = ubuf.shape[0] - tm
    pc = gbuf.shape[0] - tm

    @pl.when(t == 0)
    def _():
        ubuf[pb - nb:pb, :] = bprev_ref[0]
        gbuf[pc - nc:pc, :] = cprev_ref[0]

    h = _rms(x_ref[0], g_ref[0:1, :]).astype(BF16)

    def proj(lo, width):
        return jnp.dot(h, w_ref[:, lo:lo + width], preferred_element_type=F32)

    q = proj(0, D_A) * (HEAD_DIM ** -0.5 * LOG2_E)
    k = proj(D_A, D_A)
    v = proj(2 * D_A, D_A)
    for pg in range(tm // PAGE):
        rows = slice(pg * PAGE, (pg + 1) * PAGE)
        kt_ref[0, pg] = k[rows, :].T
        vt_ref[0, pg] = v[rows, :].T
    for p in range(N_PAIRS):
        cols = slice(p * LANES, (p + 1) * LANES)
        q_ref[0, p] = q[:, cols].astype(BF16)
        kb_ref[0, p] = k[:, cols].astype(BF16)
        vb_ref[0, p] = v[:, cols].astype(BF16)

    o = 3 * D_A
    bg = proj(o, D_B)
    cg = proj(o + D_B, D_B)
    bx = proj(o + 2 * D_B, D_B)
    ca = proj(o + 3 * D_B, D_C)
    cgt = proj(o + 3 * D_B + D_C, D_C)

    ubuf[pb:pb + tm, :] = cg * bx
    yb = jnp.zeros((tm, D_B), F32)
    for j in range(B_CONV):
        yb = yb + bw_ref[j:j + 1, :] * ubuf[pl.ds(pb - nb + j * shift, tm), :]
    obc_ref[0, :, 0:D_B] = (bg * yb).astype(BF16)
    bst_ref[0] = ubuf[pb + tm - nb:pb + tm, :]

    gbuf[pc:pc + tm, :] = ca * jax.nn.sigmoid(cgt)
    rows = min(CONV_ROWS, tm)
    for c in range(tm // rows):
        acc = jnp.zeros((rows, D_C), F32)
        for j in range(C_CONV):
            acc = acc + cw_ref[j:j + 1, :] * gbuf[pl.ds(pc - nc + j * shift + c * rows, rows), :]
        yc = acc + cb_ref[...]
        mu = jnp.mean(yc, axis=-1, keepdims=True)
        d = yc - mu
        var = jnp.mean(d * d, axis=-1, keepdims=True)
        yn = d * lax.rsqrt(var + EPS) * lg_ref[...] + lb_ref[...]
        obc_ref[0, c * rows:(c + 1) * rows, D_B:D_B + D_C] = (yn * jax.nn.sigmoid(yn)).astype(BF16)
    cst_ref[0] = gbuf[pc + tm - nc:pc + tm, :]

    if multi_tile:
        ubuf[0:pb, :] = ubuf[tm:tm + pb, :]
        gbuf[0:pc, :] = gbuf[tm:tm + pc, :]


def _inproj(x, g, w_in, bw, cw, cb, lg, lb, bprev, cprev, *, tm, shift):
    bsz, t, d = x.shape
    nt = t // tm
    assert tm % PAGE == 0
    nb = (B_CONV - 1) * shift
    nc = (C_CONV - 1) * shift
    pb = _round_up(nb, SUBLANES)
    pc = _round_up(nc, SUBLANES)
    assert t % tm == 0 and (nt == 1 or tm >= pc)
    const = lambda *shape: pl.BlockSpec(shape, lambda b, i: (0,) * len(shape))
    pair_spec = pl.BlockSpec((1, N_PAIRS, tm, LANES), lambda b, i: (b, 0, i, 0))
    row_spec = pl.BlockSpec((1, tm, D_A), lambda b, i: (b, i, 0))
    page_spec = pl.BlockSpec((1, tm // PAGE, D_A, PAGE), lambda b, i: (b, i, 0, 0))
    page_shape = jax.ShapeDtypeStruct((bsz, t // PAGE, D_A, PAGE), F32)
    pair_shape = jax.ShapeDtypeStruct((bsz, N_PAIRS, t, LANES), BF16)
    return pl.pallas_call(
        functools.partial(_inproj_kernel, tm=tm, shift=shift, multi_tile=nt > 1),
        grid=(bsz, nt),
        in_specs=[pl.BlockSpec((1, tm, d), lambda b, i: (b, i, 0)),
                  const(5, d), const(d, D_IN), const(B_CONV, D_B), const(C_CONV, D_C),
                  const(1, D_C), const(1, D_C), const(1, D_C),
                  pl.BlockSpec((1, nb, D_B), lambda b, i: (b, 0, 0)),
                  pl.BlockSpec((1, nc, D_C), lambda b, i: (b, 0, 0))],
        out_specs=[pair_spec, pair_spec, pair_spec, page_spec, page_spec, row_spec,
                   pl.BlockSpec((1, nb, D_B), lambda b, i: (b, 0, 0)),
                   pl.BlockSpec((1, nc, D_C), lambda b, i: (b, 0, 0))],
        out_shape=[pair_shape, pair_shape, pair_shape, page_shape, page_shape,
                   jax.ShapeDtypeStruct((bsz, t, D_B + D_C), BF16),
                   jax.ShapeDtypeStruct((bsz, nb, D_B), F32), jax.ShapeDtypeStruct((bsz, nc, D_C), F32)],
        scratch_shapes=[pltpu.VMEM((pb + tm, D_B), F32), pltpu.VMEM((pc + tm, D_C), F32)],
        compiler_params=_params(2),
        name="inproj",
    )(x, g, w_in, bw, cw, cb, lg, lb, bprev, cprev)


def _prompt_attn_kernel(bias_ref, q_ref, k_ref, v_ref, tri_ref, o_ref, *, tq, tk):
    p = pl.program_id(1)
    qi = pl.program_id(2)
    tri = tri_ref[...]
    n_sub = tq // tk
    row = lax.broadcasted_iota(jnp.int32, (tq, tk), 0)
    col = lax.broadcasted_iota(jnp.int32, (tq, tk), 1)
    lane_q = lax.broadcasted_iota(jnp.int32, (tq, LANES), 1)
    lane_k = lax.broadcasted_iota(jnp.int32, (tk, LANES), 1)
    heads = []
    for hh in range(HEADS_PER_LANE_TILE):
        lanes = slice(hh * HEAD_DIM, (hh + 1) * HEAD_DIM)
        other = (1 - hh) * HEAD_DIM
        own_q = (lane_q >= hh * HEAD_DIM) & (lane_q < (hh + 1) * HEAD_DIM)
        own_k = (lane_k >= hh * HEAD_DIM) & (lane_k < (hh + 1) * HEAD_DIM)
        ones = ((lane_q >= other) & (lane_q < other + BIAS_TERMS)).astype(BF16)
        bias_tile = jnp.zeros((tk, LANES), F32)
        for j in range(BIAS_TERMS):
            bias_tile = jnp.where(lane_k == other + j, bias_ref[p * HEADS_PER_LANE_TILE + hh, j], bias_tile)
        heads.append((lanes, own_k, bias_tile.astype(BF16), jnp.where(own_q, q_ref[0, 0], ones)))

    def group(first_key, n_blocks, state, on_diagonal):
        new_state = []
        for (lanes, own_k, bias_tile, qh), (carry, acc) in zip(heads, state):
            parts = []
            for s in reversed(range(n_blocks)):
                start = pl.multiple_of(first_key + s * tk, tk)
                kh = jnp.where(own_k, k_ref[0, 0, pl.ds(start, tk), :], bias_tile)
                logit2 = lax.dot_general(qh, kh, (((1,), (1,)), ((), ())), preferred_element_type=F32)
                z, sp = _score_terms(logit2)
                visible = (col + s * tk < row) if on_diagonal else None
                if on_diagonal:
                    sp = jnp.where(visible, sp, 0.0)
                parts.append((start, z, _rev_cumsum(sp, tri), visible))
            for start, z, within, visible in parts:
                a = _stick_weights(z, within, carry)
                if on_diagonal:
                    a = jnp.where(visible, a, 0.0)
                vh = v_ref[0, 0, pl.ds(start, tk), lanes]
                acc = acc + jnp.dot(a.astype(BF16), vh, preferred_element_type=F32)
                carry = carry + within[:, 0:1]
            new_state.append((carry, acc))
        return tuple(new_state)

    zero = (jnp.zeros((tq, 1), F32), jnp.zeros((tq, HEAD_DIM), F32))
    state = group(qi * tq, n_sub, (zero,) * HEADS_PER_LANE_TILE, True)
    state = lax.fori_loop(0, qi // 4, lambda i, st: group((qi - 4 - 4 * i) * tq, 4 * n_sub, st, False), state)
    state = lax.fori_loop(0, (qi % 4) // 2, lambda i, st: group((qi % 2) * tq, 2 * n_sub, st, False), state)
    state = lax.fori_loop(0, qi % 2, lambda i, st: group(0, n_sub, st, False), state)
    o_ref[0] = jnp.concatenate([acc for _, acc in state], axis=1).astype(BF16)


def _prompt_attn(q, kb, vb, bias, tri, *, tq):
    bsz, _, t, _ = q.shape
    tk = tri.shape[0]
    assert t % tq == 0 and tq % tk == 0
    kv_spec = pl.BlockSpec((1, 1, t, LANES), lambda b, p, i: (b, p, 0, 0))
    return pl.pallas_call(
        functools.partial(_prompt_attn_kernel, tq=tq, tk=tk),
        grid=(bsz, N_PAIRS, t // tq),
        in_specs=[pl.BlockSpec(memory_space=pltpu.SMEM),
                  pl.BlockSpec((1, 1, tq, LANES), lambda b, p, i: (b, p, i, 0)),
                  kv_spec, kv_spec,
                  pl.BlockSpec((tk, tk), lambda b, p, i: (0, 0))],
        out_specs=pl.BlockSpec((1, tq, LANES), lambda b, p, i: (b, i, p)),
        out_shape=jax.ShapeDtypeStruct((bsz, t, D_A), BF16),
        compiler_params=_params(3),
        name="prompt_attn",
    )(bias, q, kb, vb, tri)


def _decode_attn_kernel(pt_ref, q_ref, bias_ref, tri_ref, bd_ref, kn_ref, vn_ref, *rest, n_fetch):
    k_refs = rest[:n_fetch]
    v_refs = rest[n_fetch:2 * n_fetch]
    o_ref, acc_ref, carry_ref = rest[2 * n_fetch:]
    g = pl.program_id(1)
    q = q_ref[0]
    tri = tri_ref[...]
    n_rows = q.shape[0]

    def blocks(kv, carry, mask):
        n = len(kv)
        s2 = jnp.concatenate([jnp.dot(q, kt, preferred_element_type=F32) for kt, _ in kv], axis=0)
        z, sp = _score_terms(s2 + bias_ref[0:n * n_rows, :])
        if mask is not None:
            sp = jnp.where(mask, sp, 0.0)
        within = _rev_cumsum(sp, tri)
        carries = []
        for i in range(n):
            carries.append(carry)
            carry = carry + within[i * n_rows:(i + 1) * n_rows, 0:1]
        a = _stick_weights(z, within, jnp.concatenate(carries, axis=0))
        if mask is not None:
            a = jnp.where(mask, a, 0.0)
        a = a.astype(BF16)
        out_t = None
        for i, (_, vt) in enumerate(kv):
            part = lax.dot_general(vt, a[i * n_rows:(i + 1) * n_rows], (((1,), (1,)), ((), ())),
                                   preferred_element_type=F32)
            out_t = part if out_t is None else out_t + part
        return carry, out_t

    @pl.when(g == 0)
    def _():
        row = lax.broadcasted_iota(jnp.int32, (n_rows, PAGE), 0)
        col = lax.broadcasted_iota(jnp.int32, (n_rows, PAGE), 1)
        mask = col * N_HEADS < row - row % N_HEADS
        carry, out_t = blocks([(kn_ref[0], vn_ref[0])], jnp.zeros((n_rows, 1), F32), mask)
        carry_ref[...] = carry
        acc_ref[...] = jnp.zeros_like(acc_ref)
        acc_ref[:, 0:n_rows] = out_t

    kv = [(k_refs[i][...].astype(BF16), v_refs[i][...].astype(BF16)) for i in range(n_fetch)]
    carry, out_t = blocks(kv, carry_ref[...], None)
    carry_ref[...] = carry
    acc_ref[:, 0:n_rows] += out_t

    @pl.when(g == pl.num_programs(1) - 1)
    def _():
        own_head = acc_ref[...].T[0:n_rows, :] * bd_ref[...]
        n_q = n_rows // N_HEADS
        o_ref[0] = jnp.sum(own_head.reshape(n_q, N_HEADS, D_A), axis=1).astype(BF16)


def _decode_attn(page_table, qbd, bias_rows, tri, bdmask, knew_t, vnew_t, cache_kt, cache_vt, *, layer, n_fetch):
    n_seq, n_rows, _ = qbd.shape
    n_pages = page_table.shape[1]
    assert n_pages % n_fetch == 0 and n_rows % N_HEADS == 0
    n_q = n_rows // N_HEADS

    def page_spec(i):
        return pl.BlockSpec((None, None, D_A, PAGE),
                            lambda s, g, pt: (layer, pt[s, n_pages - 1 - (g * n_fetch + i)], 0, 0))

    const = lambda *shape: pl.BlockSpec(shape, lambda s, g, pt: (0,) * len(shape))
    seq3 = lambda *shape: pl.BlockSpec((1,) + shape, lambda s, g, pt: (s, 0, 0))
    grid_spec = pltpu.PrefetchScalarGridSpec(
        num_scalar_prefetch=1,
        grid=(n_seq, n_pages // n_fetch),
        in_specs=[seq3(n_rows, D_A), const(n_fetch * n_rows, PAGE), const(PAGE, PAGE), const(n_rows, D_A),
                  seq3(D_A, PAGE), seq3(D_A, PAGE)]
                 + [page_spec(i) for i in range(n_fetch)] * 2,
        out_specs=seq3(n_q, D_A),
        scratch_shapes=[pltpu.VMEM((D_A, LANES), F32), pltpu.VMEM((n_rows, 1), F32)],
    )
    return pl.pallas_call(
        functools.partial(_decode_attn_kernel, n_fetch=n_fetch),
        grid_spec=grid_spec,
        out_shape=jax.ShapeDtypeStruct((n_seq, n_q, D_A), BF16),
        compiler_params=_params(2),
        name="decode_attn",
    )(page_table, qbd, bias_rows, tri, bdmask, knew_t, vnew_t,
      *([cache_kt] * n_fetch), *([cache_vt] * n_fetch))


def _post_kernel(x_ref, oa_ref, obc_ref, pe_ref, g_ref, wo_ref, wg_ref, wu_ref, wd_ref, fw_ref,
                 wple_ref, wpg_ref, fprev_ref,
                 y_ref, fst_ref,
                 x1_ref, h2_ref, acc_ref, fbuf, fcar, *, tm, shift, multi_tile):
    t = pl.program_id(1)
    f = pl.program_id(2)
    nf = (FF_CONV - 1) * shift
    pf = fbuf.shape[0] - tm

    @pl.when(f == 0)
    def _():
        m = (jnp.dot(oa_ref[0], wo_ref[0:D_A, :], preferred_element_type=F32)
             + jnp.dot(obc_ref[0], wo_ref[D_A:, :], preferred_element_type=F32))
        x1 = x_ref[0] + _rms(m, g_ref[1:2, :])
        x1_ref[...] = x1
        h2_ref[...] = _rms(x1, g_ref[2:3, :]).astype(BF16)
        acc_ref[...] = jnp.zeros_like(acc_ref)

    h2 = h2_ref[...]
    fbuf[pf:pf + tm, :] = jnp.dot(h2, wg_ref[...], preferred_element_type=F32)

    @pl.when(t == 0)
    def _():
        fbuf[pf - nf:pf, :] = fprev_ref[0]

    if multi_tile:
        @pl.when(t > 0)
        def _():
            fbuf[0:pf, :] = fcar[f]

    gt = jnp.zeros((tm, fbuf.shape[1]), F32)
    for j in range(FF_CONV):
        gt = gt + fw_ref[j:j + 1, :] * fbuf[pl.ds(pf - nf + j * shift, tm), :]
    fst_ref[0] = fbuf[pf + tm - nf:pf + tm, :]
    if multi_tile:
        fcar[f] = fbuf[tm:tm + pf, :]

    up = jnp.dot(h2, wu_ref[...], preferred_element_type=F32)
    act = (jax.nn.gelu(gt, approximate=True) * up).astype(BF16)
    acc_ref[...] += jnp.dot(act, wd_ref[...], preferred_element_type=F32)

    @pl.when(f == pl.num_programs(2) - 1)
    def _():
        x2 = x1_ref[...] + _rms(acc_ref[...], g_ref[3:4, :])
        e = _rms(jnp.dot(pe_ref[0].astype(BF16), wple_ref[...], preferred_element_type=F32), g_ref[4:5, :])
        gate = jax.nn.sigmoid(jnp.dot(x2.astype(BF16), wpg_ref[...], preferred_element_type=F32))
        y_ref[0] = x2 + e * gate


def _post(x, oa, obc, pe, g, wo, wg, wu, wd, fw, wple, wpg, fprev, *, tm, shift, fc):
    bsz, t, d = x.shape
    d_ff = wg.shape[1]
    d_ple = pe.shape[2]
    nt = t // tm
    n_chunks = d_ff // fc
    nf = (FF_CONV - 1) * shift
    pf = _round_up(nf, SUBLANES)
    assert t % tm == 0 and d_ff % fc == 0 and (nt == 1 or tm >= pf)
    const = lambda *shape: pl.BlockSpec(shape, lambda b, i, f: (0,) * len(shape))
    tile = lambda width: pl.BlockSpec((1, tm, width), lambda b, i, f: (b, i, 0))
    return pl.pallas_call(
        functools.partial(_post_kernel, tm=tm, shift=shift, multi_tile=nt > 1),
        grid=(bsz, nt, n_chunks),
        in_specs=[tile(d), tile(D_A), tile(D_B + D_C), tile(d_ple),
                  const(5, d), const(D_A + D_B + D_C, d),
                  pl.BlockSpec((d, fc), lambda b, i, f: (0, f)),
                  pl.BlockSpec((d, fc), lambda b, i, f: (0, f)),
                  pl.BlockSpec((fc, d), lambda b, i, f: (f, 0)),
                  pl.BlockSpec((FF_CONV, fc), lambda b, i, f: (0, f)),
                  const(d_ple, d), const(d, d),
                  pl.BlockSpec((1, nf, fc), lambda b, i, f: (b, 0, f))],
        out_specs=[tile(d), pl.BlockSpec((1, nf, fc), lambda b, i, f: (b, 0, jnp.where(i == nt - 1, f, 0)))],
        out_shape=[jax.ShapeDtypeStruct((bsz, t, d), F32), jax.ShapeDtypeStruct((bsz, nf, d_ff), F32)],
        scratch_shapes=[pltpu.VMEM((tm, d), F32), pltpu.VMEM((tm, d), BF16), pltpu.VMEM((tm, d), F32),
                        pltpu.VMEM((pf + tm, fc), F32), pltpu.VMEM((n_chunks, pf, fc), F32)],
        compiler_params=_params(3),
        name="post",
    )(x, oa, obc, pe, g, wo, wg, wu, wd, fw, wple, wpg, fprev)


def _bf16_terms(x):
    terms = []
    rest = x
    for _ in range(BIAS_TERMS):
        term = rest.astype(BF16).astype(F32)
        terms.append(term)
        rest = rest - term
    return jnp.stack(terms, axis=1)


def _tri(n):
    j = lax.broadcasted_iota(jnp.int32, (n, n), 0)
    s = lax.broadcasted_iota(jnp.int32, (n, n), 1)
    return (j >= s).astype(BF16)


def _layer_weights(i, w_in, w_o, w_gate, w_up, w_down, w_ple, w_ple_gate):
    return tuple(w[i].astype(BF16) for w in (w_in, w_o, w_gate, w_up, w_down, w_ple, w_ple_gate))


def _time_major(a):
    n_seq, steps, c = a.shape
    return a.transpose(1, 0, 2).reshape(1, steps * n_seq, c)


def _seq_major(a, n_seq):
    _, rows, c = a.shape
    return a.reshape(rows // n_seq, n_seq, c).transpose(1, 0, 2)


def _pairs_to_seq(a, n_seq):
    steps = a.shape[2] // n_seq
    return a.reshape(N_PAIRS, steps, n_seq, LANES).transpose(2, 1, 0, 3).reshape(n_seq, steps, D_A)


def kernel(x_prompt, x_sample, p_prompt, p_sample, cache_k, cache_v, page_table, state_bconv, state_cconv, state_ffconv, w_in, w_o, sb_bias, b_conv_w, c_conv_w, c_conv_b, c_ln_g, c_ln_b, w_gate, w_up, w_down, ff_conv_w, w_ple, w_ple_gate, g_norm):
    depth = w_in.shape[0]
    bp, t, d = x_prompt.shape
    n_seq, steps, _ = x_sample.shape
    d_ff = w_gate.shape[2]
    n_pool = cache_k.shape[1]

    cache_kt = cache_k.transpose(0, 1, 3, 4, 2).reshape(depth, n_pool, D_A, PAGE)
    cache_vt = cache_v.transpose(0, 1, 3, 4, 2).reshape(depth, n_pool, D_A, PAGE)

    tri_prompt = _tri(ATTN_KEYS)
    tri_page = _tri(PAGE)
    n_rows = steps * N_HEADS
    r = lax.broadcasted_iota(jnp.int32, (n_rows, D_A), 0)
    c = lax.broadcasted_iota(jnp.int32, (n_rows, D_A), 1)
    bdmask = ((r % N_HEADS) == (c // HEAD_DIM)).astype(F32)

    xp = x_prompt
    xs = _time_major(x_sample)
    zeros = lambda rows, width: jnp.zeros((bp, rows, width), F32)
    outs = {name: [] for name in ("kp", "vp", "bp", "cp", "fp", "ks", "vs", "bs", "cs", "fs")}
    pages_out = lambda a: a.reshape(bp, t // PAGE, N_HEADS, HEAD_DIM, PAGE).transpose(0, 1, 4, 2, 3)

    for i in range(depth):
        wi, wo, wg, wu, wd, wple, wpg = _layer_weights(i, w_in, w_o, w_gate, w_up, w_down, w_ple, w_ple_gate)
        conv_args = (b_conv_w[i], c_conv_w[i], c_conv_b[i][None], c_ln_g[i][None], c_ln_b[i][None])
        bias2 = sb_bias[i] * LOG2_E
        bias2_terms = _bf16_terms(bias2)

        q, kb, vb, kt, vt, obc, bst, cst = _inproj(
            xp, g_norm[i], wi, *conv_args, zeros(B_CONV - 1, D_B), zeros(C_CONV - 1, D_C),
            tm=PROMPT_TILE, shift=1)
        oa = _prompt_attn(q, kb, vb, bias2_terms, tri_prompt, tq=ATTN_Q_ROWS)
        xp, fst = _post(xp, oa, obc, p_prompt[i], g_norm[i], wo, wg, wu, wd, ff_conv_w[i], wple, wpg,
                        zeros(FF_CONV - 1, d_ff), tm=PROMPT_TILE, shift=1, fc=FF_CHUNK)
        outs["kp"].append(pages_out(kt))
        outs["vp"].append(pages_out(vt))
        outs["bp"].append(bst)
        outs["cp"].append(cst)
        outs["fp"].append(fst)

        q, kb, vb, kt, vt, obc, bst, cst = _inproj(
            xs, g_norm[i], wi, *conv_args, _time_major(state_bconv[i]), _time_major(state_cconv[i]),
            tm=steps * n_seq, shift=n_seq)
        q_seq = _pairs_to_seq(q, n_seq)
        qbd = (jnp.repeat(q_seq, N_HEADS, axis=1) * bdmask.astype(BF16))
        pad = ((0, 0), (0, 0), (0, PAGE - steps))
        knew_t = jnp.pad(_pairs_to_seq(kb, n_seq).transpose(0, 2, 1), pad)
        vnew_t = jnp.pad(_pairs_to_seq(vb, n_seq).transpose(0, 2, 1), pad)
        bias_rows = jnp.broadcast_to(jnp.tile(bias2, steps * DECODE_PAGES)[:, None], (DECODE_PAGES * n_rows, PAGE))
        oa = _decode_attn(page_table, qbd, bias_rows, tri_page, bdmask, knew_t, vnew_t, cache_kt, cache_vt,
                          layer=i, n_fetch=DECODE_PAGES)
        xs, fst = _post(xs, _time_major(oa), obc, _time_major(p_sample[i]), g_norm[i], wo, wg, wu, wd,
                        ff_conv_w[i], wple, wpg, _time_major(state_ffconv[i]),
                        tm=steps * n_seq, shift=n_seq, fc=FF_CHUNK)
        outs["ks"].append(_seq_major(kt[:, 0].transpose(0, 2, 1), n_seq).reshape(n_seq, steps, N_HEADS, HEAD_DIM))
        outs["vs"].append(_seq_major(vt[:, 0].transpose(0, 2, 1), n_seq).reshape(n_seq, steps, N_HEADS, HEAD_DIM))
        outs["bs"].append(_seq_major(bst, n_seq))
        outs["cs"].append(_seq_major(cst, n_seq))
        outs["fs"].append(_seq_major(fst, n_seq))

    st = {name: jnp.stack(vals) for name, vals in outs.items()}
    return (xp, _seq_major(xs, n_seq), st["kp"], st["vp"], st["bp"], st["cp"], st["fp"],
            st["ks"], st["vs"], st["bs"], st["cs"], st["fs"])
```

```python
import functools

import jax
import jax.numpy as jnp
from jax import lax
from jax.experimental import pallas as pl
from jax.experimental.pallas import tpu as pltpu

F32 = jnp.float32
BF16 = jnp.bfloat16

N_HEADS = 8
HEAD_DIM = 64
D_A = N_HEADS * HEAD_DIM
D_B = 256
D_C = 256
D_IN = 3 * D_A + 3 * D_B + 2 * D_C
B_CONV = 3
C_CONV = 31
FF_CONV = 3
PAGE = 128
EPS = 1e-6
HEADS_PER_LANE_TILE = 2
N_PAIRS = N_HEADS // HEADS_PER_LANE_TILE
LANES = 128
SUBLANES = 8
VMEM_LIMIT_BYTES = 56 * 1024 * 1024
LOG2_E = 1.4426950408889634
BIAS_TERMS = 3
SCORE_LOG2_MAX = 126.0

PROMPT_TILE = 512
ATTN_Q_ROWS = 512
ATTN_KEYS = 256
FF_CHUNK = 1024
CONV_ROWS = 64
DECODE_PAGES = 32


def _round_up(a, m):
    return (a + m - 1) // m * m


def _params(n_axes):
    return pltpu.CompilerParams(dimension_semantics=("arbitrary",) * n_axes,
                                vmem_limit_bytes=VMEM_LIMIT_BYTES)


def _rms(x, g):
    return x * lax.rsqrt(jnp.mean(x * x, axis=-1, keepdims=True) + EPS) * g


def _score_terms(logit2):
    z2 = jnp.minimum(logit2, SCORE_LOG2_MAX)
    return z2, jnp.log(1.0 + jnp.exp2(z2))


def _rev_cumsum(sp, tri):
    return jnp.dot(sp.astype(BF16), tri, preferred_element_type=F32)


def _stick_weights(z2, within, carry):
    return jnp.exp2(z2 - (within + carry) * LOG2_E)


def _inproj_kernel(x_ref, g_ref, w_ref, bw_ref, cw_ref, cb_ref, lg_ref, lb_ref, bprev_ref, cprev_ref,
                   q_ref, kb_ref, vb_ref, kt_ref, vt_ref, obc_ref, bst_ref, cst_ref,
                   ubuf, gbuf, *, tm, shift, multi_tile):
    t = pl.program_id(1)
    nb = (B_CONV - 1) * shift
    nc = (C_CONV - 1) * shift
    pb = ubuf.shape[0] - tm
    pc = gbuf.shape[0] - tm

    @pl.when(t == 0)
    def _():
        ubuf[pb - nb:pb, :] = bprev_ref[0]
        gbuf[pc - nc:pc, :] = cprev_ref[0]

    h = _rms(x_ref[0], g_ref[0:1, :]).astype(BF16)

    def proj(lo, width):
        return jnp.dot(h, w_ref[:, lo:lo + width], preferred_element_type=F32)

    q = proj(0, D_A) * (HEAD_DIM ** -0.5 * LOG2_E)
    k = proj(D_A, D_A)
    v = proj(2 * D_A, D_A)
    for pg in range(tm // PAGE):
        rows = slice(pg * PAGE, (pg + 1) * PAGE)
        kt_ref[0, pg] = k[rows, :].T
        vt_ref[0, pg] = v[rows, :].T
    for p in range(N_PAIRS):
        cols = slice(p * LANES, (p + 1) * LANES)
        q_ref[0, p] = q[:, cols].astype(BF16)
        kb_ref[0, p] = k[:, cols].astype(BF16)
        vb_ref[0, p] = v[:, cols].astype(BF16)

    o = 3 * D_A
    bg = proj(o, D_B)
    cg = proj(o + D_B, D_B)
    bx = proj(o + 2 * D_B, D_B)
    ca = proj(o + 3 * D_B, D_C)
    cgt = proj(o + 3 * D_B + D_C, D_C)

    ubuf[pb:pb + tm, :] = cg * bx
    yb = jnp.zeros((tm, D_B), F32)
    for j in range(B_CONV):
        yb = yb + bw_ref[j:j + 1, :] * ubuf[pl.ds(pb - nb + j * shift, tm), :]
    obc_ref[0, :, 0:D_B] = (bg * yb).astype(BF16)
    bst_ref[0] = ubuf[pb + tm - nb:pb + tm, :]

    gbuf[pc:pc + tm, :] = ca * jax.nn.sigmoid(cgt)
    rows = min(CONV_ROWS, tm)
    for c in range(tm // rows):
        acc = jnp.zeros((rows, D_C), F32)
        for j in range(C_CONV):
            acc = acc + cw_ref[j:j + 1, :] * gbuf[pl.ds(pc - nc + j * shift + c * rows, rows), :]
        yc = acc + cb_ref[...]
        mu = jnp.mean(yc, axis=-1, keepdims=True)
        d = yc - mu
        var = jnp.mean(d * d, axis=-1, keepdims=True)
        yn = d * lax.rsqrt(var + EPS) * lg_ref[...] + lb_ref[...]
        obc_ref[0, c * rows:(c + 1) * rows, D_B:D_B + D_C] = (yn * jax.nn.sigmoid(yn)).astype(BF16)
    cst_ref[0] = gbuf[pc + tm - nc:pc + tm, :]

    if multi_tile:
        ubuf[0:pb, :] = ubuf[tm:tm + pb, :]
        gbuf[0:pc, :] = gbuf[tm:tm + pc, :]


def _inproj(x, g, w_in, bw, cw, cb, lg, lb, bprev, cprev, *, layer, tm, shift):
    bsz, t, d = x.shape
    nt = t // tm
    assert tm % PAGE == 0
    nb = (B_CONV - 1) * shift
    nc = (C_CONV - 1) * shift
    pb = _round_up(nb, SUBLANES)
    pc = _round_up(nc, SUBLANES)
    assert t % tm == 0 and (nt == 1 or tm >= pc)
    const = lambda *shape: pl.BlockSpec(shape, lambda b, i: (0,) * len(shape))
    pair_spec = pl.BlockSpec((1, N_PAIRS, tm, LANES), lambda b, i: (b, 0, i, 0))
    row_spec = pl.BlockSpec((1, tm, D_A), lambda b, i: (b, i, 0))
    page_spec = pl.BlockSpec((1, tm // PAGE, D_A, PAGE), lambda b, i: (b, i, 0, 0))
    page_shape = jax.ShapeDtypeStruct((bsz, t // PAGE, D_A, PAGE), F32)
    pair_shape = jax.ShapeDtypeStruct((bsz, N_PAIRS, t, LANES), BF16)
    return pl.pallas_call(
        functools.partial(_inproj_kernel, tm=tm, shift=shift, multi_tile=nt > 1),
        grid=(bsz, nt),
        in_specs=[pl.BlockSpec((1, tm, d), lambda b, i: (b, i, 0)),
                  const(5, d), pl.BlockSpec((None, d, D_IN), lambda b, i: (layer, 0, 0)),
                  const(B_CONV, D_B), const(C_CONV, D_C),
                  const(1, D_C), const(1, D_C), const(1, D_C),
                  pl.BlockSpec((1, nb, D_B), lambda b, i: (b, 0, 0)),
                  pl.BlockSpec((1, nc, D_C), lambda b, i: (b, 0, 0))],
        out_specs=[pair_spec, pair_spec, pair_spec, page_spec, page_spec, row_spec,
                   pl.BlockSpec((1, nb, D_B), lambda b, i: (b, 0, 0)),
                   pl.BlockSpec((1, nc, D_C), lambda b, i: (b, 0, 0))],
        out_shape=[pair_shape, pair_shape, pair_shape, page_shape, page_shape,
                   jax.ShapeDtypeStruct((bsz, t, D_B + D_C), BF16),
                   jax.ShapeDtypeStruct((bsz, nb, D_B), F32), jax.ShapeDtypeStruct((bsz, nc, D_C), F32)],
        scratch_shapes=[pltpu.VMEM((pb + tm, D_B), F32), pltpu.VMEM((pc + tm, D_C), F32)],
        compiler_params=_params(2),
        name="inproj",
    )(x, g, w_in, bw, cw, cb, lg, lb, bprev, cprev)


def _prompt_attn_kernel(bias_ref, q_ref, k_ref, v_ref, tri_ref, o_ref, *, tq, tk):
    p = pl.program_id(1)
    qi = pl.program_id(2)
    tri = tri_ref[...]
    n_sub = tq // tk
    row = lax.broadcasted_iota(jnp.int32, (tq, tk), 0)
    col = lax.broadcasted_iota(jnp.int32, (tq, tk), 1)
    lane_q = lax.broadcasted_iota(jnp.int32, (tq, LANES), 1)
    lane_k = lax.broadcasted_iota(jnp.int32, (tk, LANES), 1)
    heads = []
    for hh in range(HEADS_PER_LANE_TILE):
        lanes = slice(hh * HEAD_DIM, (hh + 1) * HEAD_DIM)
        other = (1 - hh) * HEAD_DIM
        own_q = (lane_q >= hh * HEAD_DIM) & (lane_q < (hh + 1) * HEAD_DIM)
        own_k = (lane_k >= hh * HEAD_DIM) & (lane_k < (hh + 1) * HEAD_DIM)
        ones = ((lane_q >= other) & (lane_q < other + BIAS_TERMS)).astype(BF16)
        bias_tile = jnp.zeros((tk, LANES), F32)
        for j in range(BIAS_TERMS):
            bias_tile = jnp.where(lane_k == other + j, bias_ref[p * HEADS_PER_LANE_TILE + hh, j], bias_tile)
        heads.append((lanes, own_k, bias_tile.astype(BF16), jnp.where(own_q, q_ref[0, 0], ones)))

    def group(first_key, n_blocks, state, on_diagonal):
        new_state = []
        for (lanes, own_k, bias_tile, qh), (carry, acc) in zip(heads, state):
            parts = []
            for s in reversed(range(n_blocks)):
                start = pl.multiple_of(first_key + s * tk, tk)
                kh = jnp.where(own_k, k_ref[0, 0, pl.ds(start, tk), :], bias_tile)
                logit2 = lax.dot_general(qh, kh, (((1,), (1,)), ((), ())), preferred_element_type=F32)
                z, sp = _score_terms(logit2)
                visible = (col + s * tk < row) if on_diagonal else None
                if on_diagonal:
                    sp = jnp.where(visible, sp, 0.0)
                parts.append((start, z, _rev_cumsum(sp, tri), visible))
            for start, z, within, visible in parts:
                a = _stick_weights(z, within, carry)
                if on_diagonal:
                    a = jnp.where(visible, a, 0.0)
                vh = v_ref[0, 0, pl.ds(start, tk), lanes]
                acc = acc + jnp.dot(a.astype(BF16), vh, preferred_element_type=F32)
                carry = carry + within[:, 0:1]
            new_state.append((carry, acc))
        return tuple(new_state)

    zero = (jnp.zeros((tq, 1), F32), jnp.zeros((tq, HEAD_DIM), F32))
    state = group(qi * tq, n_sub, (zero,) * HEADS_PER_LANE_TILE, True)
    state = lax.fori_loop(0, qi // 4, lambda i, st: group((qi - 4 - 4 * i) * tq, 4 * n_sub, st, False), state)
    state = lax.fori_loop(0, (qi % 4) // 2, lambda i, st: group((qi % 2) * tq, 2 * n_sub, st, False), state)
    state = lax.fori_loop(0, qi % 2, lambda i, st: group(0, n_sub, st, False), state)
    o_ref[0] = jnp.concatenate([acc for _, acc in state], axis=1).astype(BF16)


def _prompt_attn(q, kb, vb, bias, tri, *, tq):
    bsz, _, t, _ = q.shape
    tk = tri.shape[0]
    assert t % tq == 0 and tq % tk == 0
    kv_spec = pl.BlockSpec((1, 1, t, LANES), lambda b, p, i: (b, p, 0, 0))
    return pl.pallas_call(
        functools.partial(_prompt_attn_kernel, tq=tq, tk=tk),
        grid=(bsz, N_PAIRS, t // tq),
        in_specs=[pl.BlockSpec(memory_space=pltpu.SMEM),
                  pl.BlockSpec((1, 1, tq, LANES), lambda b, p, i: (b, p, i, 0)),
                  kv_spec, kv_spec,
                  pl.BlockSpec((tk, tk), lambda b, p, i: (0, 0))],
        out_specs=pl.BlockSpec((1, tq, LANES), lambda b, p, i: (b, i, p)),
        out_shape=jax.ShapeDtypeStruct((bsz, t, D_A), BF16),
        compiler_params=_params(3),
        name="prompt_attn",
    )(bias, q, kb, vb, tri)


def _decode_attn_kernel(pt_ref, q_ref, bias_ref, tri_ref, bd_ref, kn_ref, vn_ref, *rest, n_fetch):
    k_refs = rest[:n_fetch]
    v_refs = rest[n_fetch:2 * n_fetch]
    o_ref, acc_ref, carry_ref = rest[2 * n_fetch:]
    g = pl.program_id(1)
    q = q_ref[0]
    tri = tri_ref[...]
    n_rows = q.shape[0]

    def blocks(kv, carry, mask):
        n = len(kv)
        s2 = jnp.concatenate([jnp.dot(q, kt, preferred_element_type=F32) for kt, _ in kv], axis=0)
        z, sp = _score_terms(s2 + bias_ref[0:n * n_rows, :])
        if mask is not None:
            sp = jnp.where(mask, sp, 0.0)
        within = _rev_cumsum(sp, tri)
        carries = []
        for i in range(n):
            carries.append(carry)
            carry = carry + within[i * n_rows:(i + 1) * n_rows, 0:1]
        a = _stick_weights(z, within, jnp.concatenate(carries, axis=0))
        if mask is not None:
            a = jnp.where(mask, a, 0.0)
        a = a.astype(BF16)
        out_t = None
        for i, (_, vt) in enumerate(kv):
            part = lax.dot_general(vt, a[i * n_rows:(i + 1) * n_rows], (((1,), (1,)), ((), ())),
                                   preferred_element_type=F32)
            out_t = part if out_t is None else out_t + part
        return carry, out_t

    @pl.when(g == 0)
    def _():
        row = lax.broadcasted_iota(jnp.int32, (n_rows, PAGE), 0)
        col = lax.broadcasted_iota(jnp.int32, (n_rows, PAGE), 1)
        mask = col * N_HEADS < row - row % N_HEADS
        carry, out_t = blocks([(kn_ref[0], vn_ref[0])], jnp.zeros((n_rows, 1), F32), mask)
        carry_ref[...] = carry
        acc_ref[...] = jnp.zeros_like(acc_ref)
        acc_ref[:, 0:n_rows] = out_t

    kv = [(k_refs[i][...].astype(BF16), v_refs[i][...].astype(BF16)) for i in range(n_fetch)]
    carry, out_t = blocks(kv, carry_ref[...], None)
    carry_ref[...] = carry
    acc_ref[:, 0:n_rows] += out_t

    @pl.when(g == pl.num_programs(1) - 1)
    def _():
        own_head = acc_ref[...].T[0:n_rows, :] * bd_ref[...]
        n_q = n_rows // N_HEADS
        o_ref[0] = jnp.sum(own_head.reshape(n_q, N_HEADS, D_A), axis=1).astype(BF16)


def _decode_attn(page_table, qbd, bias_rows, tri, bdmask, knew_t, vnew_t, cache_kt, cache_vt, *, layer, n_fetch):
    n_seq, n_rows, _ = qbd.shape
    n_pages = page_table.shape[1]
    assert n_pages % n_fetch == 0 and n_rows % N_HEADS == 0
    n_q = n_rows // N_HEADS

    def page_spec(i):
        return pl.BlockSpec((None, None, D_A, PAGE),
                            lambda s, g, pt: (layer, pt[s, n_pages - 1 - (g * n_fetch + i)], 0, 0))

    const = lambda *shape: pl.BlockSpec(shape, lambda s, g, pt: (0,) * len(shape))
    seq3 = lambda *shape: pl.BlockSpec((1,) + shape, lambda s, g, pt: (s, 0, 0))
    grid_spec = pltpu.PrefetchScalarGridSpec(
        num_scalar_prefetch=1,
        grid=(n_seq, n_pages // n_fetch),
        in_specs=[seq3(n_rows, D_A), const(n_fetch * n_rows, PAGE), const(PAGE, PAGE), const(n_rows, D_A),
                  seq3(D_A, PAGE), seq3(D_A, PAGE)]
                 + [page_spec(i) for i in range(n_fetch)] * 2,
        out_specs=seq3(n_q, D_A),
        scratch_shapes=[pltpu.VMEM((D_A, LANES), F32), pltpu.VMEM((n_rows, 1), F32)],
    )
    return pl.pallas_call(
        functools.partial(_decode_attn_kernel, n_fetch=n_fetch),
        grid_spec=grid_spec,
        out_shape=jax.ShapeDtypeStruct((n_seq, n_q, D_A), BF16),
        compiler_params=_params(2),
        name="decode_attn",
    )(page_table, qbd, bias_rows, tri, bdmask, knew_t, vnew_t,
      *([cache_kt] * n_fetch), *([cache_vt] * n_fetch))


def _post_kernel(x_ref, oa_ref, obc_ref, pe_ref, g_ref, wo_ref, wg_ref, wu_ref, wd_ref, fw_ref,
                 wple_ref, wpg_ref, fprev_ref,
                 y_ref, fst_ref,
                 x1_ref, h2_ref, acc_ref, fbuf, fcar, *, tm, shift, multi_tile):
    t = pl.program_id(1)
    f = pl.program_id(2)
    nf = (FF_CONV - 1) * shift
    pf = fbuf.shape[0] - tm

    @pl.when(f == 0)
    def _():
        m = (jnp.dot(oa_ref[0], wo_ref[0:D_A, :], preferred_element_type=F32)
             + jnp.dot(obc_ref[0], wo_ref[D_A:, :], preferred_element_type=F32))
        x1 = x_ref[0] + _rms(m, g_ref[1:2, :])
        x1_ref[...] = x1
        h2_ref[...] = _rms(x1, g_ref[2:3, :]).astype(BF16)
        acc_ref[...] = jnp.zeros_like(acc_ref)

    h2 = h2_ref[...]
    fbuf[pf:pf + tm, :] = jnp.dot(h2, wg_ref[...], preferred_element_type=F32)

    @pl.when(t == 0)
    def _():
        fbuf[pf - nf:pf, :] = fprev_ref[0]

    if multi_tile:
        @pl.when(t > 0)
        def _():
            fbuf[0:pf, :] = fcar[f]

    gt = jnp.zeros((tm, fbuf.shape[1]), F32)
    for j in range(FF_CONV):
        gt = gt + fw_ref[j:j + 1, :] * fbuf[pl.ds(pf - nf + j * shift, tm), :]
    fst_ref[0] = fbuf[pf + tm - nf:pf + tm, :]
    if multi_tile:
        fcar[f] = fbuf[tm:tm + pf, :]

    up = jnp.dot(h2, wu_ref[...], preferred_element_type=F32)
    act = (jax.nn.gelu(gt, approximate=True) * up).astype(BF16)
    acc_ref[...] += jnp.dot(act, wd_ref[...], preferred_element_type=F32)

    @pl.when(f == pl.num_programs(2) - 1)
    def _():
        x2 = x1_ref[...] + _rms(acc_ref[...], g_ref[3:4, :])
        e = _rms(jnp.dot(pe_ref[0].astype(BF16), wple_ref[...], preferred_element_type=F32), g_ref[4:5, :])
        gate = jax.nn.sigmoid(jnp.dot(x2.astype(BF16), wpg_ref[...], preferred_element_type=F32))
        y_ref[0] = x2 + e * gate


def _post(x, oa, obc, pe, g, wo, wg, wu, wd, fw, wple, wpg, fprev, *, layer, pe_layer, tm, shift, fc):
    bsz, t, d = x.shape
    d_ff = wg.shape[2]
    d_ple = pe.shape[3]
    nt = t // tm
    n_chunks = d_ff // fc
    nf = (FF_CONV - 1) * shift
    pf = _round_up(nf, SUBLANES)
    assert t % tm == 0 and d_ff % fc == 0 and (nt == 1 or tm >= pf)
    const = lambda *shape: pl.BlockSpec(shape, lambda b, i, f: (0,) * len(shape))
    tile = lambda width: pl.BlockSpec((1, tm, width), lambda b, i, f: (b, i, 0))
    whole = lambda *shape: pl.BlockSpec((None,) + shape, lambda b, i, f: (layer,) + (0,) * len(shape))
    return pl.pallas_call(
        functools.partial(_post_kernel, tm=tm, shift=shift, multi_tile=nt > 1),
        grid=(bsz, nt, n_chunks),
        in_specs=[tile(d), tile(D_A), tile(D_B + D_C),
                  pl.BlockSpec((None, 1, tm, d_ple), lambda b, i, f: (pe_layer, b, i, 0)),
                  const(5, d), whole(D_A + D_B + D_C, d),
                  pl.BlockSpec((None, d, fc), lambda b, i, f: (layer, 0, f)),
                  pl.BlockSpec((None, d, fc), lambda b, i, f: (layer, 0, f)),
                  pl.BlockSpec((None, fc, d), lambda b, i, f: (layer, f, 0)),
                  pl.BlockSpec((FF_CONV, fc), lambda b, i, f: (0, f)),
                  whole(d_ple, d), whole(d, d),
                  pl.BlockSpec((1, nf, fc), lambda b, i, f: (b, 0, f))],
        out_specs=[tile(d), pl.BlockSpec((1, nf, fc), lambda b, i, f: (b, 0, jnp.where(i == nt - 1, f, 0)))],
        out_shape=[jax.ShapeDtypeStruct((bsz, t, d), F32), jax.ShapeDtypeStruct((bsz, nf, d_ff), F32)],
        scratch_shapes=[pltpu.VMEM((tm, d), F32), pltpu.VMEM((tm, d), BF16), pltpu.VMEM((tm, d), F32),
                        pltpu.VMEM((pf + tm, fc), F32), pltpu.VMEM((n_chunks, pf, fc), F32)],
        compiler_params=_params(3),
        name="post",
    )(x, oa, obc, pe, g, wo, wg, wu, wd, fw, wple, wpg, fprev)


def _bf16_terms(x):
    terms = []
    rest = x
    for _ in range(BIAS_TERMS):
        term = rest.astype(BF16).astype(F32)
        terms.append(term)
        rest = rest - term
    return jnp.stack(terms, axis=1)


def _tri(n):
    j = lax.broadcasted_iota(jnp.int32, (n, n), 0)
    s = lax.broadcasted_iota(jnp.int32, (n, n), 1)
    return (j >= s).astype(BF16)


def _time_major(a):
    n_seq, steps, c = a.shape
    return a.transpose(1, 0, 2).reshape(1, steps * n_seq, c)


def _seq_major(a, n_seq):
    _, rows, c = a.shape
    return a.reshape(rows // n_seq, n_seq, c).transpose(1, 0, 2)


def _pairs_to_seq(a, n_seq):
    steps = a.shape[2] // n_seq
    return a.reshape(N_PAIRS, steps, n_seq, LANES).transpose(2, 1, 0, 3).reshape(n_seq, steps, D_A)


def kernel(x_prompt, x_sample, p_prompt, p_sample, cache_k, cache_v, page_table, state_bconv, state_cconv, state_ffconv, w_in, w_o, sb_bias, b_conv_w, c_conv_w, c_conv_b, c_ln_g, c_ln_b, w_gate, w_up, w_down, ff_conv_w, w_ple, w_ple_gate, g_norm):
    depth = w_in.shape[0]
    bp, t, d = x_prompt.shape
    n_seq, steps, _ = x_sample.shape
    d_ff = w_gate.shape[2]
    n_pool = cache_k.shape[1]

    cache_kt = cache_k.transpose(0, 1, 3, 4, 2).reshape(depth, n_pool, D_A, PAGE)
    cache_vt = cache_v.transpose(0, 1, 3, 4, 2).reshape(depth, n_pool, D_A, PAGE)

    tri_prompt = _tri(ATTN_KEYS)
    tri_page = _tri(PAGE)
    n_rows = steps * N_HEADS
    r = lax.broadcasted_iota(jnp.int32, (n_rows, D_A), 0)
    c = lax.broadcasted_iota(jnp.int32, (n_rows, D_A), 1)
    bdmask = ((r % N_HEADS) == (c // HEAD_DIM)).astype(F32)

    xp = x_prompt
    xs = _time_major(x_sample)
    zeros = lambda rows, width: jnp.zeros((bp, rows, width), F32)
    outs = {name: [] for name in ("kp", "vp", "bp", "cp", "fp", "ks", "vs", "bs", "cs", "fs")}
    wi, wo, wg, wu, wd, wple, wpg = (w.astype(BF16) for w in (w_in, w_o, w_gate, w_up, w_down, w_ple, w_ple_gate))
    pages_out = lambda a: a.reshape(bp, t // PAGE, N_HEADS, HEAD_DIM, PAGE).transpose(0, 1, 4, 2, 3)

    for i in range(depth):
        conv_args = (b_conv_w[i], c_conv_w[i], c_conv_b[i][None], c_ln_g[i][None], c_ln_b[i][None])
        bias2 = sb_bias[i] * LOG2_E
        bias2_terms = _bf16_terms(bias2)

        q, kb, vb, kt, vt, obc, bst, cst = _inproj(
            xp, g_norm[i], wi, *conv_args, zeros(B_CONV - 1, D_B), zeros(C_CONV - 1, D_C),
            layer=i, tm=PROMPT_TILE, shift=1)
        oa = _prompt_attn(q, kb, vb, bias2_terms, tri_prompt, tq=ATTN_Q_ROWS)
        xp, fst = _post(xp, oa, obc, p_prompt, g_norm[i], wo, wg, wu, wd, ff_conv_w[i], wple, wpg,
                        zeros(FF_CONV - 1, d_ff), layer=i, pe_layer=i, tm=PROMPT_TILE, shift=1, fc=FF_CHUNK)
        outs["kp"].append(pages_out(kt))
        outs["vp"].append(pages_out(vt))
        outs["bp"].append(bst)
        outs["cp"].append(cst)
        outs["fp"].append(fst)

        q, kb, vb, kt, vt, obc, bst, cst = _inproj(
            xs, g_norm[i], wi, *conv_args, _time_major(state_bconv[i]), _time_major(state_cconv[i]),
            layer=i, tm=steps * n_seq, shift=n_seq)
        q_seq = _pairs_to_seq(q, n_seq)
        qbd = (jnp.repeat(q_seq, N_HEADS, axis=1) * bdmask.astype(BF16))
        pad = ((0, 0), (0, 0), (0, PAGE - steps))
        knew_t = jnp.pad(_pairs_to_seq(kb, n_seq).transpose(0, 2, 1), pad)
        vnew_t = jnp.pad(_pairs_to_seq(vb, n_seq).transpose(0, 2, 1), pad)
        bias_rows = jnp.broadcast_to(jnp.tile(bias2, steps * DECODE_PAGES)[:, None], (DECODE_PAGES * n_rows, PAGE))
        oa = _decode_attn(page_table, qbd, bias_rows, tri_page, bdmask, knew_t, vnew_t, cache_kt, cache_vt,
                          layer=i, n_fetch=DECODE_PAGES)
        xs, fst = _post(xs, _time_major(oa), obc, _time_major(p_sample[i])[None], g_norm[i], wo, wg, wu, wd,
                        ff_conv_w[i], wple, wpg, _time_major(state_ffconv[i]),
                        layer=i, pe_layer=0, tm=steps * n_seq, shift=n_seq, fc=FF_CHUNK)
        outs["ks"].append(_seq_major(kt[:, 0].transpose(0, 2, 1), n_seq).reshape(n_seq, steps, N_HEADS, HEAD_DIM))
        outs["vs"].append(_seq_major(vt[:, 0].transpose(0, 2, 1), n_seq).reshape(n_seq, steps, N_HEADS, HEAD_DIM))
        outs["bs"].append(_seq_major(bst, n_seq))
        outs["cs"].append(_seq_major(cst, n_seq))
        outs["fs"].append(_seq_major(fst, n_seq))

    st = {name: jnp.stack(vals) for name, vals in outs.items()}
    return (xp, _seq_major(xs, n_seq), st["kp"], st["vp"], st["bp"], st["cp"], st["fp"],
            st["ks"], st["vs"], st["bs"], st["cs"], st["fs"])
```

```python
import functools

import jax
import jax.numpy as jnp
from jax import lax
from jax.experimental import pallas as pl
from jax.experimental.pallas import tpu as pltpu

F32 = jnp.float32
BF16 = jnp.bfloat16

N_HEADS = 8
HEAD_DIM = 64
D_A = N_HEADS * HEAD_DIM
D_B = 256
D_C = 256
D_IN = 3 * D_A + 3 * D_B + 2 * D_C
B_CONV = 3
C_CONV = 31
FF_CONV = 3
PAGE = 128
EPS = 1e-6
HEADS_PER_LANE_TILE = 2
N_PAIRS = N_HEADS // HEADS_PER_LANE_TILE
LANES = 128
SUBLANES = 8
VMEM_LIMIT_BYTES = 56 * 1024 * 1024
LOG2_E = 1.4426950408889634
BIAS_TERMS = 3
SCORE_LOG2_MAX = 126.0

PROMPT_TILE = 512
ATTN_Q_ROWS = 512
ATTN_KEYS = 256
ATTN_GROUPS_PER_TRIP = 8
FF_CHUNK = 1024
CONV_ROWS = 64
DECODE_PAGES = 32


def _round_up(a, m):
    return (a + m - 1) // m * m


def _params(n_axes):
    return pltpu.CompilerParams(dimension_semantics=("arbitrary",) * n_axes,
                                vmem_limit_bytes=VMEM_LIMIT_BYTES)


def _rms(x, g):
    return x * lax.rsqrt(jnp.mean(x * x, axis=-1, keepdims=True) + EPS) * g


def _score_terms(logit2):
    z2 = jnp.minimum(logit2, SCORE_LOG2_MAX)
    return z2, jnp.log(1.0 + jnp.exp2(z2))


def _rev_cumsum(sp, tri):
    return jnp.dot(sp.astype(BF16), tri, preferred_element_type=F32)


def _stick_weights(z2, within, carry):
    return jnp.exp2(z2 - (within + carry) * LOG2_E)


def _inproj_kernel(x_ref, g_ref, w_ref, bw_ref, cw_ref, cb_ref, lg_ref, lb_ref, bprev_ref, cprev_ref,
                   q_ref, kb_ref, vb_ref, kt_ref, vt_ref, obc_ref, bst_ref, cst_ref,
                   ubuf, gbuf, *, tm, shift, multi_tile):
    t = pl.program_id(1)
    nb = (B_CONV - 1) * shift
    nc = (C_CONV - 1) * shift
    pb = ubuf.shape[0] - tm
    pc = gbuf.shape[0] - tm

    @pl.when(t == 0)
    def _():
        ubuf[pb - nb:pb, :] = bprev_ref[0]
        gbuf[pc - nc:pc, :] = cprev_ref[0]

    h = _rms(x_ref[0], g_ref[0:1, :]).astype(BF16)

    def proj(lo, width):
        return jnp.dot(h, w_ref[:, lo:lo + width], preferred_element_type=F32)

    q = proj(0, D_A) * (HEAD_DIM ** -0.5 * LOG2_E)
    k = proj(D_A, D_A)
    v = proj(2 * D_A, D_A)
    for pg in range(tm // PAGE):
        rows = slice(pg * PAGE, (pg + 1) * PAGE)
        kt_ref[0, pg] = k[rows, :].T
        vt_ref[0, pg] = v[rows, :].T
    for p in range(N_PAIRS):
        cols = slice(p * LANES, (p + 1) * LANES)
        q_ref[0, p] = q[:, cols].astype(BF16)
        kb_ref[0, p] = k[:, cols].astype(BF16)
        vb_ref[0, p] = v[:, cols].astype(BF16)

    o = 3 * D_A
    bg = proj(o, D_B)
    cg = proj(o + D_B, D_B)
    bx = proj(o + 2 * D_B, D_B)
    ca = proj(o + 3 * D_B, D_C)
    cgt = proj(o + 3 * D_B + D_C, D_C)

    ubuf[pb:pb + tm, :] = cg * bx
    yb = jnp.zeros((tm, D_B), F32)
    for j in range(B_CONV):
        yb = yb + bw_ref[j:j + 1, :] * ubuf[pl.ds(pb - nb + j * shift, tm), :]
    obc_ref[0, :, 0:D_B] = (bg * yb).astype(BF16)
    bst_ref[0] = ubuf[pb + tm - nb:pb + tm, :]

    gbuf[pc:pc + tm, :] = ca * jax.nn.sigmoid(cgt)
    rows = min(CONV_ROWS, tm)
    for c in range(tm // rows):
        acc = jnp.zeros((rows, D_C), F32)
        for j in range(C_CONV):
            acc = acc + cw_ref[j:j + 1, :] * gbuf[pl.ds(pc - nc + j * shift + c * rows, rows), :]
        yc = acc + cb_ref[...]
        mu = jnp.mean(yc, axis=-1, keepdims=True)
        d = yc - mu
        var = jnp.mean(d * d, axis=-1, keepdims=True)
        yn = d * lax.rsqrt(var + EPS) * lg_ref[...] + lb_ref[...]
        obc_ref[0, c * rows:(c + 1) * rows, D_B:D_B + D_C] = (yn * jax.nn.sigmoid(yn)).astype(BF16)
    cst_ref[0] = gbuf[pc + tm - nc:pc + tm, :]

    if multi_tile:
        ubuf[0:pb, :] = ubuf[tm:tm + pb, :]
        gbuf[0:pc, :] = gbuf[tm:tm + pc, :]


def _inproj(x, g, w_in, bw, cw, cb, lg, lb, bprev, cprev, *, layer, tm, shift):
    bsz, t, d = x.shape
    nt = t // tm
    assert tm % PAGE == 0
    nb = (B_CONV - 1) * shift
    nc = (C_CONV - 1) * shift
    pb = _round_up(nb, SUBLANES)
    pc = _round_up(nc, SUBLANES)
    assert t % tm == 0 and (nt == 1 or tm >= pc)
    const = lambda *shape: pl.BlockSpec(shape, lambda b, i: (0,) * len(shape))
    pair_spec = pl.BlockSpec((1, N_PAIRS, tm, LANES), lambda b, i: (b, 0, i, 0))
    row_spec = pl.BlockSpec((1, tm, D_A), lambda b, i: (b, i, 0))
    page_spec = pl.BlockSpec((1, tm // PAGE, D_A, PAGE), lambda b, i: (b, i, 0, 0))
    page_shape = jax.ShapeDtypeStruct((bsz, t // PAGE, D_A, PAGE), F32)
    pair_shape = jax.ShapeDtypeStruct((bsz, N_PAIRS, t, LANES), BF16)
    return pl.pallas_call(
        functools.partial(_inproj_kernel, tm=tm, shift=shift, multi_tile=nt > 1),
        grid=(bsz, nt),
        in_specs=[pl.BlockSpec((1, tm, d), lambda b, i: (b, i, 0)),
                  const(5, d), pl.BlockSpec((None, d, D_IN), lambda b, i: (layer, 0, 0)),
                  const(B_CONV, D_B), const(C_CONV, D_C),
                  const(1, D_C), const(1, D_C), const(1, D_C),
                  pl.BlockSpec((1, nb, D_B), lambda b, i: (b, 0, 0)),
                  pl.BlockSpec((1, nc, D_C), lambda b, i: (b, 0, 0))],
        out_specs=[pair_spec, pair_spec, pair_spec, page_spec, page_spec, row_spec,
                   pl.BlockSpec((1, nb, D_B), lambda b, i: (b, 0, 0)),
                   pl.BlockSpec((1, nc, D_C), lambda b, i: (b, 0, 0))],
        out_shape=[pair_shape, pair_shape, pair_shape, page_shape, page_shape,
                   jax.ShapeDtypeStruct((bsz, t, D_B + D_C), BF16),
                   jax.ShapeDtypeStruct((bsz, nb, D_B), F32), jax.ShapeDtypeStruct((bsz, nc, D_C), F32)],
        scratch_shapes=[pltpu.VMEM((pb + tm, D_B), F32), pltpu.VMEM((pc + tm, D_C), F32)],
        compiler_params=_params(2),
        name="inproj",
    )(x, g, w_in, bw, cw, cb, lg, lb, bprev, cprev)


def _prompt_attn_kernel(bias_ref, q_ref, k_ref, v_ref, tri_ref, o_ref, *, tq, tk):
    p = pl.program_id(1)
    qi = pl.program_id(2)
    tri = tri_ref[...]
    n_sub = tq // tk
    row = lax.broadcasted_iota(jnp.int32, (tq, tk), 0)
    col = lax.broadcasted_iota(jnp.int32, (tq, tk), 1)
    lane_q = lax.broadcasted_iota(jnp.int32, (tq, LANES), 1)
    lane_k = lax.broadcasted_iota(jnp.int32, (tk, LANES), 1)
    heads = []
    for hh in range(HEADS_PER_LANE_TILE):
        lanes = slice(hh * HEAD_DIM, (hh + 1) * HEAD_DIM)
        other = (1 - hh) * HEAD_DIM
        own_q = (lane_q >= hh * HEAD_DIM) & (lane_q < (hh + 1) * HEAD_DIM)
        own_k = (lane_k >= hh * HEAD_DIM) & (lane_k < (hh + 1) * HEAD_DIM)
        ones = ((lane_q >= other) & (lane_q < other + BIAS_TERMS)).astype(BF16)
        bias_tile = jnp.zeros((tk, LANES), F32)
        for j in range(BIAS_TERMS):
            bias_tile = jnp.where(lane_k == other + j, bias_ref[p * HEADS_PER_LANE_TILE + hh, j], bias_tile)
        heads.append((lanes, own_k, bias_tile.astype(BF16), jnp.where(own_q, q_ref[0, 0], ones)))

    def group(first_key, n_blocks, state, on_diagonal):
        new_state = []
        for (lanes, own_k, bias_tile, qh), (carry, acc) in zip(heads, state):
            parts = []
            for s in reversed(range(n_blocks)):
                start = pl.multiple_of(first_key + s * tk, tk)
                kh = jnp.where(own_k, k_ref[0, 0, pl.ds(start, tk), :], bias_tile)
                logit2 = lax.dot_general(qh, kh, (((1,), (1,)), ((), ())), preferred_element_type=F32)
                z, sp = _score_terms(logit2)
                visible = (col + s * tk < row) if on_diagonal else None
                if on_diagonal:
                    sp = jnp.where(visible, sp, 0.0)
                parts.append((start, z, _rev_cumsum(sp, tri), visible))
            for start, z, within, visible in parts:
                a = _stick_weights(z, within, carry)
                if on_diagonal:
                    a = jnp.where(visible, a, 0.0)
                vh = v_ref[0, 0, pl.ds(start, tk), lanes]
                acc = acc + jnp.dot(a.astype(BF16), vh, preferred_element_type=F32)
                carry = carry + within[:, 0:1]
            new_state.append((carry, acc))
        return tuple(new_state)

    zero = (jnp.zeros((tq, 1), F32), jnp.zeros((tq, HEAD_DIM), F32))
    state = group(qi * tq, n_sub, (zero,) * HEADS_PER_LANE_TILE, True)
    big = ATTN_GROUPS_PER_TRIP
    state = lax.fori_loop(0, qi // big, lambda i, st: group((qi - big - big * i) * tq, big * n_sub, st, False), state)
    size = big // 2
    while size >= 1:
        state = lax.fori_loop(0, (qi % (2 * size)) // size,
                              functools.partial(lambda i, st, size: group((qi % size) * tq, size * n_sub, st, False),
                                                size=size), state)
        size //= 2
    o_ref[0] = jnp.concatenate([acc for _, acc in state], axis=1).astype(BF16)


def _prompt_attn(q, kb, vb, bias, tri, *, tq):
    bsz, _, t, _ = q.shape
    tk = tri.shape[0]
    assert t % tq == 0 and tq % tk == 0
    kv_spec = pl.BlockSpec((1, 1, t, LANES), lambda b, p, i: (b, p, 0, 0))
    return pl.pallas_call(
        functools.partial(_prompt_attn_kernel, tq=tq, tk=tk),
        grid=(bsz, N_PAIRS, t // tq),
        in_specs=[pl.BlockSpec(memory_space=pltpu.SMEM),
                  pl.BlockSpec((1, 1, tq, LANES), lambda b, p, i: (b, p, i, 0)),
                  kv_spec, kv_spec,
                  pl.BlockSpec((tk, tk), lambda b, p, i: (0, 0))],
        out_specs=pl.BlockSpec((1, tq, LANES), lambda b, p, i: (b, i, p)),
        out_shape=jax.ShapeDtypeStruct((bsz, t, D_A), BF16),
        compiler_params=_params(3),
        name="prompt_attn",
    )(bias, q, kb, vb, tri)


def _decode_attn_kernel(pt_ref, q_ref, bias_ref, tri_ref, bd_ref, kn_ref, vn_ref, *rest, n_fetch):
    k_refs = rest[:n_fetch]
    v_refs = rest[n_fetch:2 * n_fetch]
    o_ref, acc_ref, carry_ref = rest[2 * n_fetch:]
    g = pl.program_id(1)
    q = q_ref[0]
    tri = tri_ref[...]
    n_rows = q.shape[0]

    def blocks(kv, carry, mask):
        n = len(kv)
        s2 = jnp.concatenate([jnp.dot(q, kt, preferred_element_type=F32) for kt, _ in kv], axis=0)
        z, sp = _score_terms(s2 + bias_ref[0:n * n_rows, :])
        if mask is not None:
            sp = jnp.where(mask, sp, 0.0)
        within = _rev_cumsum(sp, tri)
        carries = []
        for i in range(n):
            carries.append(carry)
            carry = carry + within[i * n_rows:(i + 1) * n_rows, 0:1]
        a = _stick_weights(z, within, jnp.concatenate(carries, axis=0))
        if mask is not None:
            a = jnp.where(mask, a, 0.0)
        a = a.astype(BF16)
        out_t = None
        for i, (_, vt) in enumerate(kv):
            part = lax.dot_general(vt, a[i * n_rows:(i + 1) * n_rows], (((1,), (1,)), ((), ())),
                                   preferred_element_type=F32)
            out_t = part if out_t is None else out_t + part
        return carry, out_t

    @pl.when(g == 0)
    def _():
        row = lax.broadcasted_iota(jnp.int32, (n_rows, PAGE), 0)
        col = lax.broadcasted_iota(jnp.int32, (n_rows, PAGE), 1)
        mask = col * N_HEADS < row - row % N_HEADS
        carry, out_t = blocks([(kn_ref[0], vn_ref[0])], jnp.zeros((n_rows, 1), F32), mask)
        carry_ref[...] = carry
        acc_ref[...] = jnp.zeros_like(acc_ref)
        acc_ref[:, 0:n_rows] = out_t

    kv = [(k_refs[i][...].astype(BF16), v_refs[i][...].astype(BF16)) for i in range(n_fetch)]
    carry, out_t = blocks(kv, carry_ref[...], None)
    carry_ref[...] = carry
    acc_ref[:, 0:n_rows] += out_t

    @pl.when(g == pl.num_programs(1) - 1)
    def _():
        own_head = acc_ref[...].T[0:n_rows, :] * bd_ref[...]
        n_q = n_rows // N_HEADS
        o_ref[0] = jnp.sum(own_head.reshape(n_q, N_HEADS, D_A), axis=1).astype(BF16)


def _decode_attn(page_table, qbd, bias_rows, tri, bdmask, knew_t, vnew_t, cache_kt, cache_vt, *, layer, n_fetch):
    n_seq, n_rows, _ = qbd.shape
    n_pages = page_table.shape[1]
    assert n_pages % n_fetch == 0 and n_rows % N_HEADS == 0
    n_q = n_rows // N_HEADS

    def page_spec(i):
        return pl.BlockSpec((None, None, D_A, PAGE),
                            lambda s, g, pt: (layer, pt[s, n_pages - 1 - (g * n_fetch + i)], 0, 0))

    const = lambda *shape: pl.BlockSpec(shape, lambda s, g, pt: (0,) * len(shape))
    seq3 = lambda *shape: pl.BlockSpec((1,) + shape, lambda s, g, pt: (s, 0, 0))
    grid_spec = pltpu.PrefetchScalarGridSpec(
        num_scalar_prefetch=1,
        grid=(n_seq, n_pages // n_fetch),
        in_specs=[seq3(n_rows, D_A), const(n_fetch * n_rows, PAGE), const(PAGE, PAGE), const(n_rows, D_A),
                  seq3(D_A, PAGE), seq3(D_A, PAGE)]
                 + [page_spec(i) for i in range(n_fetch)] * 2,
        out_specs=seq3(n_q, D_A),
        scratch_shapes=[pltpu.VMEM((D_A, LANES), F32), pltpu.VMEM((n_rows, 1), F32)],
    )
    return pl.pallas_call(
        functools.partial(_decode_attn_kernel, n_fetch=n_fetch),
        grid_spec=grid_spec,
        out_shape=jax.ShapeDtypeStruct((n_seq, n_q, D_A), BF16),
        compiler_params=_params(2),
        name="decode_attn",
    )(page_table, qbd, bias_rows, tri, bdmask, knew_t, vnew_t,
      *([cache_kt] * n_fetch), *([cache_vt] * n_fetch))


def _post_kernel(x_ref, oa_ref, obc_ref, pe_ref, g_ref, wo_ref, wg_ref, wu_ref, wd_ref, fw_ref,
                 wple_ref, wpg_ref, fprev_ref,
                 y_ref, fst_ref,
                 x1_ref, h2_ref, acc_ref, fbuf, fcar, *, tm, shift, multi_tile):
    t = pl.program_id(1)
    f = pl.program_id(2)
    nf = (FF_CONV - 1) * shift
    pf = fbuf.shape[0] - tm

    @pl.when(f == 0)
    def _():
        m = (jnp.dot(oa_ref[0], wo_ref[0:D_A, :], preferred_element_type=F32)
             + jnp.dot(obc_ref[0], wo_ref[D_A:, :], preferred_element_type=F32))
        x1 = x_ref[0] + _rms(m, g_ref[1:2, :])
        x1_ref[...] = x1
        h2_ref[...] = _rms(x1, g_ref[2:3, :]).astype(BF16)
        acc_ref[...] = jnp.zeros_like(acc_ref)

    h2 = h2_ref[...]
    fbuf[pf:pf + tm, :] = jnp.dot(h2, wg_ref[...], preferred_element_type=F32)

    @pl.when(t == 0)
    def _():
        fbuf[pf - nf:pf, :] = fprev_ref[0]

    if multi_tile:
        @pl.when(t > 0)
        def _():
            fbuf[0:pf, :] = fcar[f]

    gt = jnp.zeros((tm, fbuf.shape[1]), F32)
    for j in range(FF_CONV):
        gt = gt + fw_ref[j:j + 1, :] * fbuf[pl.ds(pf - nf + j * shift, tm), :]
    fst_ref[0] = fbuf[pf + tm - nf:pf + tm, :]
    if multi_tile:
        fcar[f] = fbuf[tm:tm + pf, :]

    up = jnp.dot(h2, wu_ref[...], preferred_element_type=F32)
    act = (jax.nn.gelu(gt, approximate=True) * up).astype(BF16)
    acc_ref[...] += jnp.dot(act, wd_ref[...], preferred_element_type=F32)

    @pl.when(f == pl.num_programs(2) - 1)
    def _():
        x2 = x1_ref[...] + _rms(acc_ref[...], g_ref[3:4, :])
        e = _rms(jnp.dot(pe_ref[0].astype(BF16), wple_ref[...], preferred_element_type=F32), g_ref[4:5, :])
        gate = jax.nn.sigmoid(jnp.dot(x2.astype(BF16), wpg_ref[...], preferred_element_type=F32))
        y_ref[0] = x2 + e * gate


def _post(x, oa, obc, pe, g, wo, wg, wu, wd, fw, wple, wpg, fprev, *, layer, pe_layer, tm, shift, fc):
    bsz, t, d = x.shape
    d_ff = wg.shape[2]
    d_ple = pe.shape[3]
    nt = t // tm
    n_chunks = d_ff // fc
    nf = (FF_CONV - 1) * shift
    pf = _round_up(nf, SUBLANES)
    assert t % tm == 0 and d_ff % fc == 0 and (nt == 1 or tm >= pf)
    const = lambda *shape: pl.BlockSpec(shape, lambda b, i, f: (0,) * len(shape))
    tile = lambda width: pl.BlockSpec((1, tm, width), lambda b, i, f: (b, i, 0))
    whole = lambda *shape: pl.BlockSpec((None,) + shape, lambda b, i, f: (layer,) + (0,) * len(shape))
    return pl.pallas_call(
        functools.partial(_post_kernel, tm=tm, shift=shift, multi_tile=nt > 1),
        grid=(bsz, nt, n_chunks),
        in_specs=[tile(d), tile(D_A), tile(D_B + D_C),
                  pl.BlockSpec((None, 1, tm, d_ple), lambda b, i, f: (pe_layer, b, i, 0)),
                  const(5, d), whole(D_A + D_B + D_C, d),
                  pl.BlockSpec((None, d, fc), lambda b, i, f: (layer, 0, f)),
                  pl.BlockSpec((None, d, fc), lambda b, i, f: (layer, 0, f)),
                  pl.BlockSpec((None, fc, d), lambda b, i, f: (layer, f, 0)),
                  pl.BlockSpec((FF_CONV, fc), lambda b, i, f: (0, f)),
                  whole(d_ple, d), whole(d, d),
                  pl.BlockSpec((1, nf, fc), lambda b, i, f: (b, 0, f))],
        out_specs=[tile(d), pl.BlockSpec((1, nf, fc), lambda b, i, f: (b, 0, jnp.where(i == nt - 1, f, 0)))],
        out_shape=[jax.ShapeDtypeStruct((bsz, t, d), F32), jax.ShapeDtypeStruct((bsz, nf, d_ff), F32)],
        scratch_shapes=[pltpu.VMEM((tm, d), F32), pltpu.VMEM((tm, d), BF16), pltpu.VMEM((tm, d), F32),
                        pltpu.VMEM((pf + tm, fc), F32), pltpu.VMEM((n_chunks, pf, fc), F32)],
        compiler_params=_params(3),
        name="post",
    )(x, oa, obc, pe, g, wo, wg, wu, wd, fw, wple, wpg, fprev)


def _bf16_terms(x):
    terms = []
    rest = x
    for _ in range(BIAS_TERMS):
        term = rest.astype(BF16).astype(F32)
        terms.append(term)
        rest = rest - term
    return jnp.stack(terms, axis=1)


def _tri(n):
    j = lax.broadcasted_iota(jnp.int32, (n, n), 0)
    s = lax.broadcasted_iota(jnp.int32, (n, n), 1)
    return (j >= s).astype(BF16)


def _time_major(a):
    n_seq, steps, c = a.shape
    return a.transpose(1, 0, 2).reshape(1, steps * n_seq, c)


def _seq_major(a, n_seq):
    _, rows, c = a.shape
    return a.reshape(rows // n_seq, n_seq, c).transpose(1, 0, 2)


def _pairs_to_seq(a, n_seq):
    steps = a.shape[2] // n_seq
    return a.reshape(N_PAIRS, steps, n_seq, LANES).transpose(2, 1, 0, 3).reshape(n_seq, steps, D_A)


def kernel(x_prompt, x_sample, p_prompt, p_sample, cache_k, cache_v, page_table, state_bconv, state_cconv, state_ffconv, w_in, w_o, sb_bias, b_conv_w, c_conv_w, c_conv_b, c_ln_g, c_ln_b, w_gate, w_up, w_down, ff_conv_w, w_ple, w_ple_gate, g_norm):
    depth = w_in.shape[0]
    bp, t, d = x_prompt.shape
    n_seq, steps, _ = x_sample.shape
    d_ff = w_gate.shape[2]
    n_pool = cache_k.shape[1]

    cache_kt = cache_k.transpose(0, 1, 3, 4, 2).reshape(depth, n_pool, D_A, PAGE)
    cache_vt = cache_v.transpose(0, 1, 3, 4, 2).reshape(depth, n_pool, D_A, PAGE)

    tri_prompt = _tri(ATTN_KEYS)
    tri_page = _tri(PAGE)
    n_rows = steps * N_HEADS
    r = lax.broadcasted_iota(jnp.int32, (n_rows, D_A), 0)
    c = lax.broadcasted_iota(jnp.int32, (n_rows, D_A), 1)
    bdmask = ((r % N_HEADS) == (c // HEAD_DIM)).astype(F32)

    xp = x_prompt
    xs = _time_major(x_sample)
    zeros = lambda rows, width: jnp.zeros((bp, rows, width), F32)
    outs = {name: [] for name in ("kp", "vp", "bp", "cp", "fp", "ks", "vs", "bs", "cs", "fs")}
    wi, wo, wg, wu, wd, wple, wpg = (w.astype(BF16) for w in (w_in, w_o, w_gate, w_up, w_down, w_ple, w_ple_gate))
    pages_out = lambda a: a.reshape(bp, t // PAGE, N_HEADS, HEAD_DIM, PAGE).transpose(0, 1, 4, 2, 3)

    for i in range(depth):
        conv_args = (b_conv_w[i], c_conv_w[i], c_conv_b[i][None], c_ln_g[i][None], c_ln_b[i][None])
        bias2 = sb_bias[i] * LOG2_E
        bias2_terms = _bf16_terms(bias2)

        q, kb, vb, kt, vt, obc, bst, cst = _inproj(
            xp, g_norm[i], wi, *conv_args, zeros(B_CONV - 1, D_B), zeros(C_CONV - 1, D_C),
            layer=i, tm=PROMPT_TILE, shift=1)
        oa = _prompt_attn(q, kb, vb, bias2_terms, tri_prompt, tq=ATTN_Q_ROWS)
        xp, fst = _post(xp, oa, obc, p_prompt, g_norm[i], wo, wg, wu, wd, ff_conv_w[i], wple, wpg,
                        zeros(FF_CONV - 1, d_ff), layer=i, pe_layer=i, tm=PROMPT_TILE, shift=1, fc=FF_CHUNK)
        outs["kp"].append(pages_out(kt))
        outs["vp"].append(pages_out(vt))
        outs["bp"].append(bst)
        outs["cp"].append(cst)
        outs["fp"].append(fst)

        q, kb, vb, kt, vt, obc, bst, cst = _inproj(
            xs, g_norm[i], wi, *conv_args, _time_major(state_bconv[i]), _time_major(state_cconv[i]),
            layer=i, tm=steps * n_seq, shift=n_seq)
        q_seq = _pairs_to_seq(q, n_seq)
        qbd = (jnp.repeat(q_seq, N_HEADS, axis=1) * bdmask.astype(BF16))
        pad = ((0, 0), (0, 0), (0, PAGE - steps))
        knew_t = jnp.pad(_pairs_to_seq(kb, n_seq).transpose(0, 2, 1), pad)
        vnew_t = jnp.pad(_pairs_to_seq(vb, n_seq).transpose(0, 2, 1), pad)
        bias_rows = jnp.broadcast_to(jnp.tile(bias2, steps * DECODE_PAGES)[:, None], (DECODE_PAGES * n_rows, PAGE))
        oa = _decode_attn(page_table, qbd, bias_rows, tri_page, bdmask, knew_t, vnew_t, cache_kt, cache_vt,
                          layer=i, n_fetch=DECODE_PAGES)
        xs, fst = _post(xs, _time_major(oa), obc, _time_major(p_sample[i])[None], g_norm[i], wo, wg, wu, wd,
                        ff_conv_w[i], wple, wpg, _time_major(state_ffconv[i]),
                        layer=i, pe_layer=0, tm=steps * n_seq, shift=n_seq, fc=FF_CHUNK)
        outs["ks"].append(_seq_major(kt[:, 0].transpose(0, 2, 1), n_seq).reshape(n_seq, steps, N_HEADS, HEAD_DIM))
        outs["vs"].append(_seq_major(vt[:, 0].transpose(0, 2, 1), n_seq).reshape(n_seq, steps, N_HEADS, HEAD_DIM))
        outs["bs"].append(_seq_major(bst, n_seq))
        outs["cs"].append(_seq_major(cst, n_seq))
        outs["fs"].append(_seq_major(fst, n_seq))

    st = {name: jnp.stack(vals) for name, vals in outs.items()}
    return (xp, _seq_major(xs, n_seq), st["kp"], st["vp"], st["bp"], st["cp"], st["fp"],
            st["ks"], st["vs"], st["bs"], st["cs"], st["fs"])
```
